```python
import jax, jax.numpy as jnp
from jax import lax
import numpy as np

D_MODEL = 1024
BATCH = 4
SEQ = 4096
DEPTH = 2

N_META = 16
N_MIXERS = 2
SB_HEADS = 16
SB_HEAD_DIM = D_MODEL // SB_HEADS
Q_BLOCK = 128
LRU_WIDTH = D_MODEL
LRU_BLOCKS = 8
LRU_BLOCK_DIM = LRU_WIDTH // LRU_BLOCKS
CONV_WIDTH = 4
LRU_C = 8.0
D_FF = 4 * D_MODEL
EPS = 1e-6
N_SB_LAYERS = (DEPTH + 1) // 2
N_LRU_LAYERS = DEPTH // 2

kernel_name = "hybrid_stickbreak_rglru_meta"


def rms_norm(x, g):
    xf = x.astype(jnp.float32)
    y = xf * lax.rsqrt(jnp.mean(xf * xf, axis=-1, keepdims=True) + EPS)
    return (y * g.astype(jnp.float32)).astype(x.dtype)


def _sb_block(q_blk, k, v, q_pos):
    t_len = k.shape[2]
    k_pos = jnp.arange(t_len)
    z = jnp.einsum('bhqd,bhkd->bhqk', q_blk, k).astype(jnp.float32) * (SB_HEAD_DIM ** -0.5)
    causal = k_pos[None, :] < q_pos[:, None]
    log_keep = jnp.where(causal, -jax.nn.softplus(z), 0.0)
    after = lax.cumsum(log_keep, axis=3, reverse=True) - log_keep
    w = jnp.where(causal, jnp.exp(jax.nn.log_sigmoid(z) + after), 0.0)
    return jnp.einsum('bhqk,bhkd->bhqd', w.astype(v.dtype), v)


def stick_breaking_mixer(x, w_qkv, w_o):
    b, t_len, _ = x.shape
    qkv = (x @ w_qkv).reshape(b, t_len, 3, SB_HEADS, SB_HEAD_DIM).transpose(2, 0, 3, 1, 4)
    q, k, v = qkv[0], qkv[1], qkv[2]
    meta_out = _sb_block(q[:, :, :N_META], k, v, jnp.arange(N_META))
    n_blk = (t_len - N_META) // Q_BLOCK
    q_real = q[:, :, N_META:].reshape(b, SB_HEADS, n_blk, Q_BLOCK, SB_HEAD_DIM).transpose(2, 0, 1, 3, 4)
    pos = N_META + jnp.arange(n_blk * Q_BLOCK).reshape(n_blk, Q_BLOCK)
    real_out = lax.map(lambda a: _sb_block(a[0], k, v, a[1]), (q_real, pos))
    real_out = real_out.transpose(1, 2, 0, 3, 4).reshape(b, SB_HEADS, n_blk * Q_BLOCK, SB_HEAD_DIM)
    o = jnp.concatenate([meta_out, real_out], axis=2)
    o = o.transpose(0, 2, 1, 3).reshape(b, t_len, D_MODEL)
    return o @ w_o


def _lru_combine(left, right):
    a1, b1 = left
    a2, b2 = right
    return a1 * a2, a2 * b1 + b2


def rglru_mixer(x, w_in, conv_w, conv_b, w_rg, b_rg, w_ig, b_ig, lam, w_out):
    b, t_len, _ = x.shape
    gate_in, rec_in = jnp.split(x @ w_in, 2, axis=-1)
    gate = jax.nn.gelu(gate_in)
    xp = jnp.pad(rec_in, ((0, 0), (CONV_WIDTH - 1, 0), (0, 0)))
    u = conv_b + sum(xp[:, j:j + t_len] * conv_w[j] for j in range(CONV_WIDTH))
    ub = u.reshape(b, t_len, LRU_BLOCKS, LRU_BLOCK_DIM)
    r = jax.nn.sigmoid(jnp.einsum('btni,nij->btnj', ub, w_rg).reshape(b, t_len, LRU_WIDTH) + b_rg)
    i = jax.nn.sigmoid(jnp.einsum('btni,nij->btnj', ub, w_ig).reshape(b, t_len, LRU_WIDTH) + b_ig)
    log_a = (-LRU_C * jax.nn.softplus(-lam.astype(jnp.float32))) * r.astype(jnp.float32)
    a = jnp.exp(log_a)
    mult = jnp.sqrt(-jnp.expm1(2.0 * log_a))
    bt = mult * (i * u).astype(jnp.float32)
    _, h = lax.associative_scan(_lru_combine, (a, bt), axis=1)
    y = h.astype(x.dtype) * gate
    return y @ w_out


def sq_relu_mlp(x, w_up, w_down):
    hdn = jax.nn.relu(x @ w_up)
    return (hdn * hdn) @ w_down


def setup_inputs(seed: int = 0) -> dict:
    key = jax.random.key(seed)
    ks = jax.random.split(key, 20)
    f32 = jnp.float32
    D = D_MODEL

    def nrm(k, shape, scale):
        return jax.random.normal(k, shape, f32) * scale

    a0 = jax.random.uniform(ks[11], (N_LRU_LAYERS, LRU_WIDTH), f32, 0.9, 0.999)
    return {
        "x": nrm(ks[0], (BATCH, SEQ, D), 1.0),
        "meta_tokens": nrm(ks[1], (N_META, D), 1.0),
        "norm_mix": 1.0 + nrm(ks[2], (DEPTH, D), 0.02),
        "norm_mlp": 1.0 + nrm(ks[3], (DEPTH, D), 0.02),
        "sb_w_qkv": nrm(ks[4], (N_SB_LAYERS, D, 3 * D), D ** -0.5),
        "sb_w_o": nrm(ks[5], (N_SB_LAYERS, D, D), D ** -0.5),
        "lru_w_in": nrm(ks[6], (N_LRU_LAYERS, D, 2 * LRU_WIDTH), D ** -0.5),
        "lru_conv_w": nrm(ks[7], (N_LRU_LAYERS, CONV_WIDTH, LRU_WIDTH), CONV_WIDTH ** -0.5),
        "lru_conv_b": nrm(ks[8], (N_LRU_LAYERS, LRU_WIDTH), 0.01),
        "lru_w_rg": nrm(ks[9], (N_LRU_LAYERS, LRU_BLOCKS, LRU_BLOCK_DIM, LRU_BLOCK_DIM), LRU_BLOCK_DIM ** -0.5),
        "lru_b_rg": nrm(ks[10], (N_LRU_LAYERS, LRU_WIDTH), 0.01),
        "lru_w_ig": nrm(ks[12], (N_LRU_LAYERS, LRU_BLOCKS, LRU_BLOCK_DIM, LRU_BLOCK_DIM), LRU_BLOCK_DIM ** -0.5),
        "lru_b_ig": nrm(ks[13], (N_LRU_LAYERS, LRU_WIDTH), 0.01),
        "lru_lambda": jnp.log(a0) - jnp.log1p(-a0),
        "lru_w_out": nrm(ks[14], (N_LRU_LAYERS, LRU_WIDTH, D), LRU_WIDTH ** -0.5),
        "mlp_w_up": nrm(ks[15], (DEPTH, D, D_FF), D ** -0.5),
        "mlp_w_down": nrm(ks[16], (DEPTH, D_FF, D), D_FF ** -0.5),
        "norm_final": 1.0 + nrm(ks[17], (D,), 0.02),
    }


def reference(x, meta_tokens, norm_mix, norm_mlp, sb_w_qkv, sb_w_o, lru_w_in, lru_conv_w,
              lru_conv_b, lru_w_rg, lru_b_rg, lru_w_ig, lru_b_ig, lru_lambda, lru_w_out,
              mlp_w_up, mlp_w_down, norm_final):
    b = x.shape[0]
    meta = jnp.broadcast_to(meta_tokens[None].astype(x.dtype), (b, N_META, D_MODEL))
    h = jnp.concatenate([meta, x], axis=1)
    for i in range(DEPTH):
        hn = rms_norm(h, norm_mix[i])
        j = i // N_MIXERS
        if i % N_MIXERS == 0:
            h = h + stick_breaking_mixer(hn, sb_w_qkv[j], sb_w_o[j])
        else:
            h = h + rglru_mixer(hn, lru_w_in[j], lru_conv_w[j], lru_conv_b[j], lru_w_rg[j],
                                lru_b_rg[j], lru_w_ig[j], lru_b_ig[j], lru_lambda[j], lru_w_out[j])
        hn = rms_norm(h, norm_mlp[i])
        h = h + sq_relu_mlp(hn, mlp_w_up[i], mlp_w_down[i])
    h = rms_norm(h, norm_final)
    return h[:, N_META:]
```

```python
import functools
import math

import jax
import jax.numpy as jnp
from jax import lax
from jax.experimental import pallas as pl
from jax.experimental.pallas import tpu as pltpu

N_META = 16
HEADS = 16
HEAD_DIM = 64
LRU_BLOCKS = 8
LRU_BLOCK_DIM = 128
CONV_WIDTH = 4
LRU_C = 8.0
EPS = 1e-6
LOG2E = 1.4426950408889634

LANES = 128
SUBLANES = 8
TIME_BLOCK = 384
ATTN_BLOCK = 128
MLP_ROWS = 512
MLP_FF_CHUNK = 1024
VMEM_LIMIT = 56 * 1024 * 1024

_BF16 = jnp.bfloat16
_F32 = jnp.float32


def _dot(a, b):
    return jnp.dot(a, b, preferred_element_type=_F32)


def _rms_norm(x, g):
    ms = jnp.mean(x * x, axis=-1, keepdims=True)
    return (x * lax.rsqrt(ms + EPS)) * g


def _resident(shape):
    zeros = (0,) * len(shape)
    return pl.BlockSpec(shape, lambda *_: zeros, pipeline_mode=pl.Buffered(1))


def _qkv_kernel(h_ref, g_ref, wq_ref, wkt_ref, wv_ref, q_ref, kt_ref, v_ref):
    hn = _rms_norm(h_ref[...], g_ref[...]).astype(_BF16)
    q_ref[...] = (_dot(hn, wq_ref[...]) * (HEAD_DIM ** -0.5)).astype(_BF16)
    v_ref[...] = _dot(hn, wv_ref[...]).astype(_BF16)
    kt = lax.dot_general(wkt_ref[...], hn, (((1,), (1,)), ((), ())), preferred_element_type=_F32)
    kt_ref[...] = kt.astype(_BF16)


def _qkv(h, g, wq, wkt, wv):
    b, tp, d = h.shape
    nt = tp // TIME_BLOCK
    row_spec = pl.BlockSpec((None, TIME_BLOCK, d), lambda i, j: (i, j, 0))
    return pl.pallas_call(
        _qkv_kernel,
        grid=(b, nt),
        in_specs=[row_spec, _resident((1, d)), _resident((d, d)), _resident((d, d)), _resident((d, d))],
        out_specs=[row_spec, pl.BlockSpec((None, d, TIME_BLOCK), lambda i, j: (i, 0, j)), row_spec],
        out_shape=[jax.ShapeDtypeStruct((b, tp, d), _BF16), jax.ShapeDtypeStruct((b, d, tp), _BF16),
                   jax.ShapeDtypeStruct((b, tp, d), _BF16)],
        compiler_params=pltpu.CompilerParams(dimension_semantics=("parallel", "parallel"),
                                             vmem_limit_bytes=VMEM_LIMIT),
        name="qkv",
    )(h, g, wq, wkt, wv)


def _attn_kernel(q_ref, kt_ref, v_ref, nu_ref, o_ref, acc_ref, carry_ref):
    blk = ATTN_BLOCK
    nq = q_ref.shape[0] // blk
    lane = lax.broadcasted_iota(jnp.int32, (blk, LANES), 1)
    row = lax.broadcasted_iota(jnp.int32, (blk, blk), 0)
    col = lax.broadcasted_iota(jnp.int32, (blk, blk), 1)
    causal = col < row
    head_lanes = (lane < HEAD_DIM, lane >= HEAD_DIM)
    nu = nu_ref[...]

    def tile(qh, kb, diagonal):
        start = pl.multiple_of(kb * blk, blk)
        kt = kt_ref[:, pl.ds(start, blk)]
        v = v_ref[pl.ds(start, blk), :]
        for hd in range(2):
            z = _dot(qh[hd], kt) * LOG2E
            e = jnp.exp2(-jnp.abs(z))
            sp = jnp.maximum(z, 0.0) + jnp.log2(1.0 + e)
            if diagonal:
                sp = jnp.where(causal, sp, 0.0)
            hi = sp.astype(_BF16)
            lo = (sp - hi.astype(_F32)).astype(_BF16)
            sums = _dot(jnp.concatenate([hi, lo], axis=1), nu)
            if diagonal:
                w = jnp.where(causal, jnp.exp2(z + sums[:, :blk]), 0.0)
                acc_ref[hd] = _dot(w.astype(_BF16), v)
                carry_ref[hd] = sums[:, blk:]
            else:
                w = jnp.exp2(z + sums[:, :blk] + carry_ref[hd])
                acc_ref[hd] += _dot(w.astype(_BF16), v)
                carry_ref[hd] += sums[:, blk:]

    def q_block(qi, _):
        q = q_ref[pl.ds(pl.multiple_of(qi * blk, blk), blk), :]
        qh = [jnp.where(m, q, jnp.zeros_like(q)) for m in head_lanes]
        tile(qh, qi, True)

        def key_block(i, _):
            tile(qh, qi - 1 - i, False)
            return 0

        lax.fori_loop(0, qi, key_block, 0)
        out = jnp.where(head_lanes[0], acc_ref[0], acc_ref[1])
        o_ref[pl.ds(pl.multiple_of(qi * blk, blk), blk), :] = out.astype(o_ref.dtype)
        return 0

    lax.fori_loop(0, nq, q_block, 0)


def _suffix_matrix():
    blk = ATTN_BLOCK
    j = jnp.arange(2 * blk)[:, None] % blk
    s = jnp.arange(2 * blk)[None, :]
    nu = jnp.where(s < blk, (j >= s).astype(_F32), 1.0)
    return (-nu).astype(_BF16)


def _sb_attention(q, kt, v):
    b, tp, d = q.shape
    pairs = d // LANES
    return pl.pallas_call(
        _attn_kernel,
        grid=(b, pairs),
        in_specs=[pl.BlockSpec((None, tp, LANES), lambda i, j: (i, 0, j)),
                  pl.BlockSpec((None, LANES, tp), lambda i, j: (i, j, 0)),
                  pl.BlockSpec((None, tp, LANES), lambda i, j: (i, 0, j)),
                  _resident((2 * ATTN_BLOCK, 2 * ATTN_BLOCK))],
        out_specs=pl.BlockSpec((None, tp, LANES), lambda i, j: (i, 0, j)),
        out_shape=jax.ShapeDtypeStruct((b, tp, d), _BF16),
        scratch_shapes=[pltpu.VMEM((2, ATTN_BLOCK, LANES), _F32), pltpu.VMEM((2, ATTN_BLOCK, ATTN_BLOCK), _F32)],
        compiler_params=pltpu.CompilerParams(dimension_semantics=("parallel", "parallel"),
                                             vmem_limit_bytes=VMEM_LIMIT),
        name="sb_attn",
    )(q, kt, v, _suffix_matrix())


def _gelu_tanh(x):
    return 0.5 * x * (1.0 + jnp.tanh(math.sqrt(2.0 / math.pi) * (x + 0.044715 * (x * x * x))))


def _sigmoid(x):
    return 1.0 / (1.0 + jnp.exp(-x))


def _lru_kernel(h_ref, g_ref, win_ref, cw_ref, cb_ref, wg_ref, brg_ref, big_ref, lam_ref, y_ref,
                rec_ref, a_ref, b_ref, state_ref):
    rows, d = h_ref.shape
    tail = SUBLANES

    @pl.when(pl.program_id(1) == 0)
    def _():
        rec_ref[0:tail, :] = jnp.zeros((tail, d), _F32)
        state_ref[...] = jnp.zeros_like(state_ref)

    hn = _rms_norm(h_ref[...], g_ref[...]).astype(_BF16)
    gate = _gelu_tanh(_dot(hn, win_ref[:, :d]))
    rec_ref[tail:, :] = _dot(hn, win_ref[:, d:])

    u = cb_ref[...] + cw_ref[CONV_WIDTH - 1:CONV_WIDTH, :] * rec_ref[tail:, :]
    for j in range(CONV_WIDTH - 1):
        shift = CONV_WIDTH - 1 - j
        u = u + cw_ref[j:j + 1, :] * rec_ref[tail - shift:tail - shift + rows, :]
    rec_ref[0:tail, :] = rec_ref[rows:rows + tail, :]

    neg_lam = -lam_ref[...]
    log_a_unit = -LRU_C * (jnp.maximum(neg_lam, 0.0) + jnp.log1p(jnp.exp(-jnp.abs(neg_lam))))
    ub = u.astype(_BF16)
    for n in range(LRU_BLOCKS):
        cols = slice(n * LRU_BLOCK_DIM, (n + 1) * LRU_BLOCK_DIM)
        ri = _dot(ub[:, cols], wg_ref[n])
        r = _sigmoid(ri[:, :LRU_BLOCK_DIM] + brg_ref[:, cols])
        i = _sigmoid(ri[:, LRU_BLOCK_DIM:] + big_ref[:, cols])
        log_a = log_a_unit[:, cols] * r
        a = jnp.exp(log_a)
        mult = jnp.sqrt(-jnp.tanh(log_a) * (1.0 + a * a))
        a_ref[:, cols] = a
        b_ref[:, cols] = mult * (i * u[:, cols])

    sub = lax.broadcasted_iota(jnp.int32, (SUBLANES, d), 0)

    def group(gidx, state):
        r0 = pl.multiple_of(gidx * SUBLANES, SUBLANES)
        av = a_ref[pl.ds(r0, SUBLANES), :]
        bv = b_ref[pl.ds(r0, SUBLANES), :]
        for s in (1, 2, 4):
            keep = sub >= s
            bv = jnp.where(keep, av * pltpu.roll(bv, s, axis=0) + bv, bv)
            av = jnp.where(keep, av * pltpu.roll(av, s, axis=0), av)
        hv = av * state + bv
        b_ref[pl.ds(r0, SUBLANES), :] = hv
        return jnp.broadcast_to(hv[SUBLANES - 1:SUBLANES, :], (SUBLANES, d))

    state_ref[...] = lax.fori_loop(0, rows // SUBLANES, group, state_ref[...])
    y_ref[...] = (b_ref[...] * gate).astype(y_ref.dtype)


def _lru_mixer(h, g, w_in, conv_w, conv_b, w_gates, b_rg, b_ig, lam):
    b, tp, d = h.shape
    nt = tp // TIME_BLOCK
    row_spec = pl.BlockSpec((None, TIME_BLOCK, d), lambda i, j: (i, j, 0))
    return pl.pallas_call(
        _lru_kernel,
        grid=(b, nt),
        in_specs=[row_spec, _resident((1, d)), _resident((d, 2 * d)), _resident((CONV_WIDTH, d)),
                  _resident((1, d)), _resident((LRU_BLOCKS, LRU_BLOCK_DIM, 2 * LRU_BLOCK_DIM)),
                  _resident((1, d)), _resident((1, d)), _resident((1, d))],
        out_specs=row_spec,
        out_shape=jax.ShapeDtypeStruct((b, tp, d), _BF16),
        scratch_shapes=[pltpu.VMEM((SUBLANES + TIME_BLOCK, d), _F32), pltpu.VMEM((TIME_BLOCK, d), _F32),
                        pltpu.VMEM((TIME_BLOCK, d), _F32), pltpu.VMEM((SUBLANES, d), _F32)],
        compiler_params=pltpu.CompilerParams(dimension_semantics=("parallel", "arbitrary"),
                                             vmem_limit_bytes=VMEM_LIMIT),
        name="lru",
    )(h, g, w_in, conv_w, conv_b, w_gates, b_rg, b_ig, lam)


def _mlp_kernel(y_ref, h_ref, wo_ref, g_ref, wup_ref, wdn_ref, gf_ref, o_ref, hn_ref, *, final_norm):
    d_ff = wup_ref.shape[1]
    h1 = h_ref[...] + _dot(y_ref[...], wo_ref[...])
    hn_ref[...] = _rms_norm(h1, g_ref[...]).astype(_BF16)
    o_ref[...] = h1
    for c in range(0, d_ff, MLP_FF_CHUNK):
        up = jnp.maximum(_dot(hn_ref[...], wup_ref[:, c:c + MLP_FF_CHUNK]), 0.0)
        o_ref[...] += _dot((up * up).astype(_BF16), wdn_ref[c:c + MLP_FF_CHUNK, :])
    if final_norm:
        o_ref[...] = _rms_norm(o_ref[...], gf_ref[...])


def _proj_mlp(y, h, w_o, g, w_up, w_down, g_final, final_norm):
    m, d = h.shape
    d_ff = w_up.shape[1]
    row_spec = pl.BlockSpec((MLP_ROWS, d), lambda i: (i, 0))
    return pl.pallas_call(
        functools.partial(_mlp_kernel, final_norm=final_norm),
        grid=(m // MLP_ROWS,),
        in_specs=[row_spec, row_spec, _resident((d, d)), _resident((1, d)), _resident((d, d_ff)),
                  _resident((d_ff, d)), _resident((1, d))],
        out_specs=row_spec,
        out_shape=jax.ShapeDtypeStruct((m, d), _F32),
        scratch_shapes=[pltpu.VMEM((MLP_ROWS, d), _BF16)],
        compiler_params=pltpu.CompilerParams(dimension_semantics=("parallel",), vmem_limit_bytes=VMEM_LIMIT),
        name="proj_mlp",
    )(y, h, w_o, g, w_up, w_down, g_final)


def kernel(x, meta_tokens, norm_mix, norm_mlp, sb_w_qkv, sb_w_o, lru_w_in, lru_conv_w, lru_conv_b, lru_w_rg,
           lru_b_rg, lru_w_ig, lru_b_ig, lru_lambda, lru_w_out, mlp_w_up, mlp_w_down, norm_final):
    b, seq, d = x.shape
    assert d == HEADS * HEAD_DIM == LRU_BLOCKS * LRU_BLOCK_DIM
    t_len = N_META + seq
    tp = -(-t_len // TIME_BLOCK) * TIME_BLOCK
    assert (b * tp) % MLP_ROWS == 0 and tp % ATTN_BLOCK == 0

    meta = jnp.broadcast_to(meta_tokens[None].astype(x.dtype), (b, N_META, d))
    h = jnp.concatenate([meta, x, jnp.zeros((b, tp - t_len, d), x.dtype)], axis=1)

    row = lambda v: v.reshape(1, d)
    w_qkv = sb_w_qkv[0]
    q, kt, v = _qkv(h, row(norm_mix[0]), w_qkv[:, :d].astype(_BF16), w_qkv[:, d:2 * d].T.astype(_BF16),
                    w_qkv[:, 2 * d:].astype(_BF16))
    o = _sb_attention(q, kt, v)
    h = _proj_mlp(o.reshape(b * tp, d), h.reshape(b * tp, d), sb_w_o[0].astype(_BF16), row(norm_mlp[0]),
                  mlp_w_up[0].astype(_BF16), mlp_w_down[0].astype(_BF16), row(norm_final), False)

    w_gates = jnp.concatenate([lru_w_rg[0], lru_w_ig[0]], axis=-1).astype(_BF16)
    y = _lru_mixer(h.reshape(b, tp, d), row(norm_mix[1]), lru_w_in[0].astype(_BF16), lru_conv_w[0],
                   row(lru_conv_b[0]), w_gates, row(lru_b_rg[0]), row(lru_b_ig[0]), row(lru_lambda[0]))
    h = _proj_mlp(y.reshape(b * tp, d), h, lru_w_out[0].astype(_BF16), row(norm_mlp[1]),
                  mlp_w_up[1].astype(_BF16), mlp_w_down[1].astype(_BF16), row(norm_final), True)
    return h.reshape(b, tp, d)[:, N_META:t_len]
```

```python
import functools
import math

import jax
import jax.numpy as jnp
from jax import lax
from jax.experimental import pallas as pl
from jax.experimental.pallas import tpu as pltpu

N_META = 16
HEADS = 16
HEAD_DIM = 64
LRU_BLOCKS = 8
LRU_BLOCK_DIM = 128
CONV_WIDTH = 4
LRU_C = 8.0
EPS = 1e-6
LOG2E = 1.4426950408889634

LANES = 128
SUBLANES = 8
TIME_BLOCK = 384
ATTN_BLOCK = 128
ATTN_Q_ROWS = 512
ATTN_KEY_GROUP = 4
MLP_ROWS = 512
MLP_FF_CHUNK = 1024
VMEM_LIMIT = 56 * 1024 * 1024

_BF16 = jnp.bfloat16
_F32 = jnp.float32


def _dot(a, b):
    return jnp.dot(a, b, preferred_element_type=_F32)


def _rms_norm(x, g):
    ms = jnp.mean(x * x, axis=-1, keepdims=True)
    return (x * lax.rsqrt(ms + EPS)) * g


def _resident(shape):
    zeros = (0,) * len(shape)
    return pl.BlockSpec(shape, lambda *_: zeros, pipeline_mode=pl.Buffered(1))


def _qkv_kernel(h_ref, g_ref, wq_ref, wkt_ref, wv_ref, q_ref, kt_ref, v_ref):
    hn = _rms_norm(h_ref[...], g_ref[...]).astype(_BF16)
    q_ref[...] = (_dot(hn, wq_ref[...]) * (HEAD_DIM ** -0.5)).astype(_BF16)
    v_ref[...] = _dot(hn, wv_ref[...]).astype(_BF16)
    kt = lax.dot_general(wkt_ref[...], hn, (((1,), (1,)), ((), ())), preferred_element_type=_F32)
    kt_ref[...] = kt.astype(_BF16)


def _qkv(h, g, wq, wkt, wv):
    b, tp, d = h.shape
    nt = tp // TIME_BLOCK
    row_spec = pl.BlockSpec((None, TIME_BLOCK, d), lambda i, j: (i, j, 0))
    return pl.pallas_call(
        _qkv_kernel,
        grid=(b, nt),
        in_specs=[row_spec, _resident((1, d)), _resident((d, d)), _resident((d, d)), _resident((d, d))],
        out_specs=[row_spec, pl.BlockSpec((None, d, TIME_BLOCK), lambda i, j: (i, 0, j)), row_spec],
        out_shape=[jax.ShapeDtypeStruct((b, tp, d), _BF16), jax.ShapeDtypeStruct((b, d, tp), _BF16),
                   jax.ShapeDtypeStruct((b, tp, d), _BF16)],
        compiler_params=pltpu.CompilerParams(dimension_semantics=("parallel", "parallel"),
                                             vmem_limit_bytes=VMEM_LIMIT),
        name="qkv",
    )(h, g, wq, wkt, wv)


def _attn_kernel(q_ref, kt_ref, v_ref, nu_ref, o_ref, acc_ref, carry_ref):
    blk, qrows, group = ATTN_BLOCK, ATTN_Q_ROWS, ATTN_KEY_GROUP
    n_wide = (q_ref.shape[0] - blk) // qrows
    nu = nu_ref[...]

    def lane_index(rows):
        return lax.broadcasted_iota(jnp.int32, (rows, LANES), 1)

    def causal(rows):
        return lane_index(rows) < lax.broadcasted_iota(jnp.int32, (rows, LANES), 0)

    def sweep(qh, row_lo, rows, key_blk, n_blk, diagonal):
        k0 = pl.multiple_of(key_blk * blk, blk)
        kt = kt_ref[:, pl.ds(k0, n_blk * blk)]
        v = v_ref[pl.ds(k0, n_blk * blk), :]
        mask = causal(rows) if diagonal else None
        for hd in range(2):
            z = _dot(qh[hd][row_lo:row_lo + rows], kt) * LOG2E
            sp = jnp.maximum(jnp.log2(1.0 + jnp.exp2(jnp.minimum(z, 100.0))), z)
            if diagonal:
                sp = jnp.where(mask, sp, 0.0)
            sp = sp.astype(_BF16)
            carry = carry_ref[hd, row_lo:row_lo + rows, :]
            ws = [None] * n_blk
            for i in reversed(range(n_blk)):
                cols = slice(i * blk, (i + 1) * blk)
                suffix = _dot(sp[:, cols], nu)
                w = jnp.exp2(z[:, cols] + suffix + carry)
                if diagonal:
                    w = jnp.where(mask, w, 0.0)
                ws[i] = w.astype(_BF16)
                carry = carry + jnp.broadcast_to(suffix[:, 0:1], (rows, LANES))
            carry_ref[hd, row_lo:row_lo + rows, :] = carry
            w_all = ws[0] if n_blk == 1 else jnp.concatenate(ws, axis=1)
            acc_ref[hd, row_lo:row_lo + rows, :] += _dot(w_all, v)

    def q_block(row0, rows, first_key_blk):
        q = q_ref[pl.ds(row0, rows), :]
        first_head = lane_index(rows) < HEAD_DIM
        qh = [jnp.where(first_head, q, jnp.zeros_like(q)), jnp.where(first_head, jnp.zeros_like(q), q)]
        acc_ref[...] = jnp.zeros_like(acc_ref)
        carry_ref[...] = jnp.zeros_like(carry_ref)
        n_diag = rows // blk
        for j in reversed(range(n_diag)):
            sweep(qh, j * blk, rows - j * blk, first_key_blk + j, 1, True)
        return qh

    def write(row0, rows):
        out = jnp.where(lane_index(rows) < HEAD_DIM, acc_ref[0, :rows], acc_ref[1, :rows])
        o_ref[pl.ds(row0, rows), :] = out.astype(o_ref.dtype)

    q_block(0, blk, 0)
    write(0, blk)

    def wide_block(qi, _):
        row0 = pl.multiple_of(blk + qi * qrows, blk)
        first = 1 + qi * (qrows // blk)
        qh = q_block(row0, qrows, first)
        sweep(qh, 0, qrows, first - 1, 1, False)

        def key_group(g, _):
            sweep(qh, 0, qrows, first - 1 - (g + 1) * group, group, False)
            return 0

        lax.fori_loop(0, (first - 1) // group, key_group, 0)
        write(row0, qrows)
        return 0

    lax.fori_loop(0, n_wide, wide_block, 0)


def _suffix_matrix():
    j = jnp.arange(ATTN_BLOCK)[:, None]
    s = jnp.arange(ATTN_BLOCK)[None, :]
    return -(j >= s).astype(_BF16)


def _sb_attention(q, kt, v):
    b, tp, d = q.shape
    assert (tp - ATTN_BLOCK) % ATTN_Q_ROWS == 0 and (ATTN_Q_ROWS // ATTN_BLOCK) % ATTN_KEY_GROUP == 0
    pairs = d // LANES
    return pl.pallas_call(
        _attn_kernel,
        grid=(b, pairs),
        in_specs=[pl.BlockSpec((None, tp, LANES), lambda i, j: (i, 0, j)),
                  pl.BlockSpec((None, LANES, tp), lambda i, j: (i, j, 0)),
                  pl.BlockSpec((None, tp, LANES), lambda i, j: (i, 0, j)),
                  _resident((ATTN_BLOCK, ATTN_BLOCK))],
        out_specs=pl.BlockSpec((None, tp, LANES), lambda i, j: (i, 0, j)),
        out_shape=jax.ShapeDtypeStruct((b, tp, d), _BF16),
        scratch_shapes=[pltpu.VMEM((2, ATTN_Q_ROWS, LANES), _F32), pltpu.VMEM((2, ATTN_Q_ROWS, LANES), _F32)],
        compiler_params=pltpu.CompilerParams(dimension_semantics=("parallel", "parallel"),
                                             vmem_limit_bytes=VMEM_LIMIT),
        name="sb_attn",
    )(q, kt, v, _suffix_matrix())


def _gelu_tanh(x):
    return 0.5 * x * (1.0 + jnp.tanh(math.sqrt(2.0 / math.pi) * (x + 0.044715 * (x * x * x))))


def _sigmoid(x):
    return 1.0 / (1.0 + jnp.exp(-x))


def _lru_kernel(h_ref, g_ref, win_ref, cw_ref, cb_ref, wg_ref, brg_ref, big_ref, lam_ref, y_ref,
                rec_ref, a_ref, b_ref, state_ref):
    rows, d = h_ref.shape
    tail = SUBLANES

    @pl.when(pl.program_id(1) == 0)
    def _():
        rec_ref[0:tail, :] = jnp.zeros((tail, d), _F32)
        state_ref[...] = jnp.zeros_like(state_ref)

    hn = _rms_norm(h_ref[...], g_ref[...]).astype(_BF16)
    gate = _gelu_tanh(_dot(hn, win_ref[:, :d]))
    rec_ref[tail:, :] = _dot(hn, win_ref[:, d:])

    u = cb_ref[...] + cw_ref[CONV_WIDTH - 1:CONV_WIDTH, :] * rec_ref[tail:, :]
    for j in range(CONV_WIDTH - 1):
        shift = CONV_WIDTH - 1 - j
        u = u + cw_ref[j:j + 1, :] * rec_ref[tail - shift:tail - shift + rows, :]
    rec_ref[0:tail, :] = rec_ref[rows:rows + tail, :]

    neg_lam = -lam_ref[...]
    log_a_unit = -LRU_C * (jnp.maximum(neg_lam, 0.0) + jnp.log1p(jnp.exp(-jnp.abs(neg_lam))))
    ub = u.astype(_BF16)
    for n in range(LRU_BLOCKS):
        cols = slice(n * LRU_BLOCK_DIM, (n + 1) * LRU_BLOCK_DIM)
        ri = _dot(ub[:, cols], wg_ref[n])
        r = _sigmoid(ri[:, :LRU_BLOCK_DIM] + brg_ref[:, cols])
        i = _sigmoid(ri[:, LRU_BLOCK_DIM:] + big_ref[:, cols])
        log_a = log_a_unit[:, cols] * r
        a = jnp.exp(log_a)
        mult = jnp.sqrt(-jnp.tanh(log_a) * (1.0 + a * a))
        a_ref[:, cols] = a
        b_ref[:, cols] = mult * (i * u[:, cols])

    sub = lax.broadcasted_iota(jnp.int32, (SUBLANES, d), 0)

    def group(gidx, state):
        r0 = pl.multiple_of(gidx * SUBLANES, SUBLANES)
        av = a_ref[pl.ds(r0, SUBLANES), :]
        bv = b_ref[pl.ds(r0, SUBLANES), :]
        for s in (1, 2, 4):
            keep = sub >= s
            bv = jnp.where(keep, av * pltpu.roll(bv, s, axis=0) + bv, bv)
            av = jnp.where(keep, av * pltpu.roll(av, s, axis=0), av)
        hv = av * state + bv
        b_ref[pl.ds(r0, SUBLANES), :] = hv
        return jnp.broadcast_to(hv[SUBLANES - 1:SUBLANES, :], (SUBLANES, d))

    state_ref[...] = lax.fori_loop(0, rows // SUBLANES, group, state_ref[...])
    y_ref[...] = (b_ref[...] * gate).astype(y_ref.dtype)


def _lru_mixer(h, g, w_in, conv_w, conv_b, w_gates, b_rg, b_ig, lam):
    b, tp, d = h.shape
    nt = tp // TIME_BLOCK
    row_spec = pl.BlockSpec((None, TIME_BLOCK, d), lambda i, j: (i, j, 0))
    return pl.pallas_call(
        _lru_kernel,
        grid=(b, nt),
        in_specs=[row_spec, _resident((1, d)), _resident((d, 2 * d)), _resident((CONV_WIDTH, d)),
                  _resident((1, d)), _resident((LRU_BLOCKS, LRU_BLOCK_DIM, 2 * LRU_BLOCK_DIM)),
                  _resident((1, d)), _resident((1, d)), _resident((1, d))],
        out_specs=row_spec,
        out_shape=jax.ShapeDtypeStruct((b, tp, d), _BF16),
        scratch_shapes=[pltpu.VMEM((SUBLANES + TIME_BLOCK, d), _F32), pltpu.VMEM((TIME_BLOCK, d), _F32),
                        pltpu.VMEM((TIME_BLOCK, d), _F32), pltpu.VMEM((SUBLANES, d), _F32)],
        compiler_params=pltpu.CompilerParams(dimension_semantics=("parallel", "arbitrary"),
                                             vmem_limit_bytes=VMEM_LIMIT),
        name="lru",
    )(h, g, w_in, conv_w, conv_b, w_gates, b_rg, b_ig, lam)


def _mlp_kernel(y_ref, h_ref, wo_ref, g_ref, wup_ref, wdn_ref, gf_ref, o_ref, hn_ref, *, final_norm):
    d_ff = wup_ref.shape[1]
    h1 = h_ref[...] + _dot(y_ref[...], wo_ref[...])
    hn_ref[...] = _rms_norm(h1, g_ref[...]).astype(_BF16)
    o_ref[...] = h1
    for c in range(0, d_ff, MLP_FF_CHUNK):
        up = jnp.maximum(_dot(hn_ref[...], wup_ref[:, c:c + MLP_FF_CHUNK]), 0.0)
        o_ref[...] += _dot((up * up).astype(_BF16), wdn_ref[c:c + MLP_FF_CHUNK, :])
    if final_norm:
        o_ref[...] = _rms_norm(o_ref[...], gf_ref[...])


def _proj_mlp(y, h, w_o, g, w_up, w_down, g_final, final_norm):
    m, d = h.shape
    d_ff = w_up.shape[1]
    row_spec = pl.BlockSpec((MLP_ROWS, d), lambda i: (i, 0))
    return pl.pallas_call(
        functools.partial(_mlp_kernel, final_norm=final_norm),
        grid=(m // MLP_ROWS,),
        in_specs=[row_spec, row_spec, _resident((d, d)), _resident((1, d)), _resident((d, d_ff)),
                  _resident((d_ff, d)), _resident((1, d))],
        out_specs=row_spec,
        out_shape=jax.ShapeDtypeStruct((m, d), _F32),
        scratch_shapes=[pltpu.VMEM((MLP_ROWS, d), _BF16)],
        compiler_params=pltpu.CompilerParams(dimension_semantics=("parallel",), vmem_limit_bytes=VMEM_LIMIT),
        name="proj_mlp",
    )(y, h, w_o, g, w_up, w_down, g_final)


def kernel(x, meta_tokens, norm_mix, norm_mlp, sb_w_qkv, sb_w_o, lru_w_in, lru_conv_w, lru_conv_b, lru_w_rg,
           lru_b_rg, lru_w_ig, lru_b_ig, lru_lambda, lru_w_out, mlp_w_up, mlp_w_down, norm_final):
    b, seq, d = x.shape
    assert d == HEADS * HEAD_DIM == LRU_BLOCKS * LRU_BLOCK_DIM
    t_len = N_META + seq
    tp = -(-t_len // TIME_BLOCK) * TIME_BLOCK
    assert (b * tp) % MLP_ROWS == 0 and tp % ATTN_BLOCK == 0

    meta = jnp.broadcast_to(meta_tokens[None].astype(x.dtype), (b, N_META, d))
    h = jnp.concatenate([meta, x, jnp.zeros((b, tp - t_len, d), x.dtype)], axis=1)

    row = lambda v: v.reshape(1, d)
    w_qkv = sb_w_qkv[0]
    q, kt, v = _qkv(h, row(norm_mix[0]), w_qkv[:, :d].astype(_BF16), w_qkv[:, d:2 * d].T.astype(_BF16),
                    w_qkv[:, 2 * d:].astype(_BF16))
    o = _sb_attention(q, kt, v)
    h = _proj_mlp(o.reshape(b * tp, d), h.reshape(b * tp, d), sb_w_o[0].astype(_BF16), row(norm_mlp[0]),
                  mlp_w_up[0].astype(_BF16), mlp_w_down[0].astype(_BF16), row(norm_final), False)

    w_gates = jnp.concatenate([lru_w_rg[0], lru_w_ig[0]], axis=-1).astype(_BF16)
    y = _lru_mixer(h.reshape(b, tp, d), row(norm_mix[1]), lru_w_in[0].astype(_BF16), lru_conv_w[0],
                   row(lru_conv_b[0]), w_gates, row(lru_b_rg[0]), row(lru_b_ig[0]), row(lru_lambda[0]))
    h = _proj_mlp(y.reshape(b * tp, d), h, lru_w_out[0].astype(_BF16), row(norm_mlp[1]),
                  mlp_w_up[1].astype(_BF16), mlp_w_down[1].astype(_BF16), row(norm_final), True)
    return h.reshape(b, tp, d)[:, N_META:t_len]
```

```python
import functools
import math

import jax
import jax.numpy as jnp
from jax import lax
from jax.experimental import pallas as pl
from jax.experimental.pallas import tpu as pltpu

N_META = 16
HEADS = 16
HEAD_DIM = 64
LRU_BLOCKS = 8
LRU_BLOCK_DIM = 128
CONV_WIDTH = 4
LRU_C = 8.0
EPS = 1e-6
LOG2E = 1.4426950408889634

LANES = 128
SUBLANES = 8
TIME_BLOCK = 384
ATTN_BLOCK = 128
ATTN_Q_ROWS = 512
ATTN_KEY_GROUP = 4
ATTN_SUFFIX_BLOCK = 256
ATTN_HEADS_PER_STEP = 4
MLP_ROWS = 512
MLP_FF_CHUNK = 1024
VMEM_LIMIT = 56 * 1024 * 1024

_BF16 = jnp.bfloat16
_F32 = jnp.float32


def _dot(a, b):
    return jnp.dot(a, b, preferred_element_type=_F32)


def _rms_norm(x, g):
    ms = jnp.mean(x * x, axis=-1, keepdims=True)
    return (x * lax.rsqrt(ms + EPS)) * g


def _resident(shape):
    zeros = (0,) * len(shape)
    return pl.BlockSpec(shape, lambda *_: zeros, pipeline_mode=pl.Buffered(1))


def _qkv_kernel(h_ref, g_ref, wq_ref, wkt_ref, wv_ref, q_ref, kt_ref, v_ref):
    hn = _rms_norm(h_ref[...], g_ref[...]).astype(_BF16)
    q_ref[...] = (_dot(hn, wq_ref[...]) * (HEAD_DIM ** -0.5 * LOG2E)).astype(_BF16)
    v_ref[...] = _dot(hn, wv_ref[...]).astype(_BF16)
    kt = lax.dot_general(wkt_ref[...], hn, (((1,), (1,)), ((), ())), preferred_element_type=_F32)
    kt_ref[...] = kt.astype(_BF16)


def _qkv(h, g, wq, wkt, wv):
    b, tp, d = h.shape
    nt = tp // TIME_BLOCK
    row_spec = pl.BlockSpec((None, TIME_BLOCK, d), lambda i, j: (i, j, 0))
    return pl.pallas_call(
        _qkv_kernel,
        grid=(b, nt),
        in_specs=[row_spec, _resident((1, d)), _resident((d, d)), _resident((d, d)), _resident((d, d))],
        out_specs=[row_spec, pl.BlockSpec((None, d, TIME_BLOCK), lambda i, j: (i, 0, j)), row_spec],
        out_shape=[jax.ShapeDtypeStruct((b, tp, d), _BF16), jax.ShapeDtypeStruct((b, d, tp), _BF16),
                   jax.ShapeDtypeStruct((b, tp, d), _BF16)],
        compiler_params=pltpu.CompilerParams(dimension_semantics=("parallel", "parallel"),
                                             vmem_limit_bytes=VMEM_LIMIT),
        name="qkv",
    )(h, g, wq, wkt, wv)


def _attn_kernel(q_ref, kt_ref, v_ref, nu_ref, o_ref, acc_ref, carry_ref, z_ref, w_ref):
    blk, qrows, group = ATTN_BLOCK, ATTN_Q_ROWS, ATTN_KEY_GROUP
    lanes = q_ref.shape[1]
    n_heads = lanes // HEAD_DIM

    def head_of_lane(rows):
        return lax.broadcasted_iota(jnp.int32, (rows, lanes), 1) // HEAD_DIM

    def stacked_values(key_blk, keys):
        v = v_ref[pl.ds(pl.multiple_of(key_blk * blk, blk), keys), :]
        v_head = head_of_lane(keys)
        return jnp.concatenate([jnp.where(v_head == hd, v, jnp.zeros_like(v)) for hd in range(n_heads)], axis=0)

    def scores(q_head, key_blk, keys):
        return _dot(q_head, kt_ref[:, pl.ds(pl.multiple_of(key_blk * blk, blk), keys)])

    def visibility(rows, n_blk, causal_shift):
        masks = []
        for k_idx in range(n_blk):
            if causal_shift is None or (k_idx + 1) * blk - causal_shift <= 0:
                masks.append(None)
            else:
                key_pos = lax.broadcasted_iota(jnp.int32, (rows, blk), 1) + (k_idx * blk - causal_shift)
                masks.append(key_pos < lax.broadcasted_iota(jnp.int32, (rows, blk), 0))
        return masks

    def weights(z, hd, rows, widths, masks):
        keys = sum(widths)
        sp = jnp.maximum(jnp.log2(1.0 + jnp.exp2(jnp.minimum(z, 100.0))), z)
        sp = jnp.concatenate([sp[:, k * blk:(k + 1) * blk] if m is None else
                              jnp.where(m, sp[:, k * blk:(k + 1) * blk], 0.0)
                              for k, m in enumerate(masks)], axis=1).astype(_BF16)
        carry = carry_ref[hd, :rows, :]
        w_head = [None] * len(masks)
        lo = keys
        for width in reversed(widths):
            lo -= width
            suffix = _dot(sp[:, lo:lo + width], nu_ref[:width, :width])
            for c in range(width // blk):
                k_idx = lo // blk + c
                w = jnp.exp2(z[:, k_idx * blk:(k_idx + 1) * blk] + suffix[:, c * blk:(c + 1) * blk] + carry)
                if masks[k_idx] is not None:
                    w = jnp.where(masks[k_idx], w, 0.0)
                w_head[k_idx] = w.astype(_BF16)
            carry = carry + jnp.broadcast_to(suffix[:, 0:1], (rows, LANES))
        carry_ref[hd, :rows, :] = carry
        return w_head

    def sweep(qh, rows, key_blk, widths, causal_shift):
        keys = sum(widths)
        masks = visibility(rows, keys // blk, causal_shift)
        ws = []
        for hd in range(n_heads):
            ws += weights(scores(qh[hd], key_blk, keys), hd, rows, widths, masks)
        acc_ref[:rows, :] += _dot(jnp.concatenate(ws, axis=1), stacked_values(key_blk, keys))

    def start_block(row0, rows):
        q = q_ref[pl.ds(row0, rows), :]
        q_head = head_of_lane(rows)
        acc_ref[...] = jnp.zeros_like(acc_ref)
        carry_ref[...] = jnp.zeros_like(carry_ref)
        return [jnp.where(q_head == hd, q, jnp.zeros_like(q)) for hd in range(n_heads)]

    def write(row0, rows):
        o_ref[pl.ds(row0, rows), :] = acc_ref[:rows, :].astype(o_ref.dtype)

    sweep(start_block(0, blk), blk, 0, [blk], 0)
    write(0, blk)

    group_widths = [ATTN_SUFFIX_BLOCK] * (group * blk // ATTN_SUFFIX_BLOCK)
    group_keys = group * blk
    assert group_keys == qrows

    def wide_block(qi, _):
        row0 = pl.multiple_of(blk + qi * qrows, blk)
        first = 1 + qi * group
        qh = start_block(row0, qrows)

        def item_start(g):
            return jnp.maximum(first - (g + 1) * group, 0)

        def store_weights(w_head, hd):
            w_ref[:, hd * group_keys:(hd + 1) * group_keys] = jnp.concatenate(w_head, axis=1)

        def pv(g):
            acc_ref[...] += _dot(w_ref[...], stacked_values(item_start(g), group_keys))

        diag_masks = visibility(qrows, group, 0)
        for hd in range(n_heads):
            store_weights(weights(scores(qh[hd], first, group_keys), hd, qrows, group_widths, diag_masks), hd)
            z_ref[hd] = scores(qh[hd], item_start(0), group_keys)

        def key_group(g, _):
            pv(g - 1)
            for hd in range(n_heads):
                store_weights(weights(z_ref[hd], hd, qrows, group_widths, [None] * group), hd)
                z_ref[hd] = scores(qh[hd], item_start(g + 1), group_keys)
            return 0

        lax.fori_loop(0, qi, key_group, 0)
        pv(qi - 1)
        sweep(qh, qrows, 0, [blk], None)
        write(row0, qrows)
        return 0

    lax.fori_loop(0, (q_ref.shape[0] - blk) // qrows, wide_block, 0)


def _suffix_matrix():
    j = jnp.arange(ATTN_SUFFIX_BLOCK)[:, None]
    s = jnp.arange(ATTN_SUFFIX_BLOCK)[None, :]
    return -(j >= s).astype(_BF16)


def _sb_attention(q, kt, v):
    b, tp, d = q.shape
    assert (tp - ATTN_BLOCK) % ATTN_Q_ROWS == 0 and (ATTN_Q_ROWS // ATTN_BLOCK) % ATTN_KEY_GROUP == 0
    lanes = ATTN_HEADS_PER_STEP * HEAD_DIM
    return pl.pallas_call(
        _attn_kernel,
        grid=(b, d // lanes),
        in_specs=[pl.BlockSpec((None, tp, lanes), lambda i, j: (i, 0, j)),
                  pl.BlockSpec((None, lanes, tp), lambda i, j: (i, j, 0)),
                  pl.BlockSpec((None, tp, lanes), lambda i, j: (i, 0, j)),
                  _resident((ATTN_SUFFIX_BLOCK, ATTN_SUFFIX_BLOCK))],
        out_specs=pl.BlockSpec((None, tp, lanes), lambda i, j: (i, 0, j)),
        out_shape=jax.ShapeDtypeStruct((b, tp, d), _BF16),
        scratch_shapes=[pltpu.VMEM((ATTN_Q_ROWS, lanes), _F32),
                        pltpu.VMEM((ATTN_HEADS_PER_STEP, ATTN_Q_ROWS, LANES), _F32),
                        pltpu.VMEM((ATTN_HEADS_PER_STEP, ATTN_Q_ROWS, ATTN_KEY_GROUP * ATTN_BLOCK), _F32),
                        pltpu.VMEM((ATTN_Q_ROWS, ATTN_HEADS_PER_STEP * ATTN_KEY_GROUP * ATTN_BLOCK), _BF16)],
        compiler_params=pltpu.CompilerParams(dimension_semantics=("parallel", "parallel"),
                                             vmem_limit_bytes=VMEM_LIMIT),
        name="sb_attn",
    )(q, kt, v, _suffix_matrix())


def _gelu_tanh(x):
    return 0.5 * x * (1.0 + jnp.tanh(math.sqrt(2.0 / math.pi) * (x + 0.044715 * (x * x * x))))


def _sigmoid(x):
    return 1.0 / (1.0 + jnp.exp(-x))


def _lru_kernel(h_ref, g_ref, win_ref, cw_ref, cb_ref, wg_ref, brg_ref, big_ref, lam_ref, y_ref,
                rec_ref, a_ref, b_ref, state_ref):
    rows, d = h_ref.shape
    tail = SUBLANES

    @pl.when(pl.program_id(1) == 0)
    def _():
        rec_ref[0:tail, :] = jnp.zeros((tail, d), _F32)
        state_ref[...] = jnp.zeros_like(state_ref)

    hn = _rms_norm(h_ref[...], g_ref[...]).astype(_BF16)
    gate = _gelu_tanh(_dot(hn, win_ref[:, :d]))
    rec_ref[tail:, :] = _dot(hn, win_ref[:, d:])

    u = cb_ref[...] + cw_ref[CONV_WIDTH - 1:CONV_WIDTH, :] * rec_ref[tail:, :]
    for j in range(CONV_WIDTH - 1):
        shift = CONV_WIDTH - 1 - j
        u = u + cw_ref[j:j + 1, :] * rec_ref[tail - shift:tail - shift + rows, :]
    rec_ref[0:tail, :] = rec_ref[rows:rows + tail, :]

    neg_lam = -lam_ref[...]
    log_a_unit = -LRU_C * (jnp.maximum(neg_lam, 0.0) + jnp.log1p(jnp.exp(-jnp.abs(neg_lam))))
    ub = u.astype(_BF16)
    for n in range(LRU_BLOCKS):
        cols = slice(n * LRU_BLOCK_DIM, (n + 1) * LRU_BLOCK_DIM)
        ri = _dot(ub[:, cols], wg_ref[n])
        r = _sigmoid(ri[:, :LRU_BLOCK_DIM] + brg_ref[:, cols])
        i = _sigmoid(ri[:, LRU_BLOCK_DIM:] + big_ref[:, cols])
        log_a = log_a_unit[:, cols] * r
        a = jnp.exp(log_a)
        mult = jnp.sqrt(-jnp.tanh(log_a) * (1.0 + a * a))
        a_ref[:, cols] = a
        b_ref[:, cols] = mult * (i * u[:, cols])

    sub = lax.broadcasted_iota(jnp.int32, (SUBLANES, d), 0)

    def group(gidx, state):
        r0 = pl.multiple_of(gidx * SUBLANES, SUBLANES)
        av = a_ref[pl.ds(r0, SUBLANES), :]
        bv = b_ref[pl.ds(r0, SUBLANES), :]
        for s in (1, 2, 4):
            keep = sub >= s
            bv = jnp.where(keep, av * pltpu.roll(bv, s, axis=0) + bv, bv)
            av = jnp.where(keep, av * pltpu.roll(av, s, axis=0), av)
        hv = av * state + bv
        b_ref[pl.ds(r0, SUBLANES), :] = hv
        return jnp.broadcast_to(hv[SUBLANES - 1:SUBLANES, :], (SUBLANES, d))

    state_ref[...] = lax.fori_loop(0, rows // SUBLANES, group, state_ref[...])
    y_ref[...] = (b_ref[...] * gate).astype(y_ref.dtype)


def _lru_mixer(h, g, w_in, conv_w, conv_b, w_gates, b_rg, b_ig, lam):
    b, tp, d = h.shape
    nt = tp // TIME_BLOCK
    row_spec = pl.BlockSpec((None, TIME_BLOCK, d), lambda i, j: (i, j, 0))
    return pl.pallas_call(
        _lru_kernel,
        grid=(b, nt),
        in_specs=[row_spec, _resident((1, d)), _resident((d, 2 * d)), _resident((CONV_WIDTH, d)),
                  _resident((1, d)), _resident((LRU_BLOCKS, LRU_BLOCK_DIM, 2 * LRU_BLOCK_DIM)),
                  _resident((1, d)), _resident((1, d)), _resident((1, d))],
        out_specs=row_spec,
        out_shape=jax.ShapeDtypeStruct((b, tp, d), _BF16),
        scratch_shapes=[pltpu.VMEM((SUBLANES + TIME_BLOCK, d), _F32), pltpu.VMEM((TIME_BLOCK, d), _F32),
                        pltpu.VMEM((TIME_BLOCK, d), _F32), pltpu.VMEM((SUBLANES, d), _F32)],
        compiler_params=pltpu.CompilerParams(dimension_semantics=("parallel", "arbitrary"),
                                             vmem_limit_bytes=VMEM_LIMIT),
        name="lru",
    )(h, g, w_in, conv_w, conv_b, w_gates, b_rg, b_ig, lam)


def _mlp_kernel(y_ref, h_ref, wo_ref, g_ref, wup_ref, wdn_ref, gf_ref, o_ref, hn_ref, *, final_norm):
    d_ff = wup_ref.shape[1]
    h1 = h_ref[...] + _dot(y_ref[...], wo_ref[...])
    hn_ref[...] = _rms_norm(h1, g_ref[...]).astype(_BF16)
    o_ref[...] = h1
    for c in range(0, d_ff, MLP_FF_CHUNK):
        up = jnp.maximum(_dot(hn_ref[...], wup_ref[:, c:c + MLP_FF_CHUNK]), 0.0)
        o_ref[...] += _dot((up * up).astype(_BF16), wdn_ref[c:c + MLP_FF_CHUNK, :])
    if final_norm:
        o_ref[...] = _rms_norm(o_ref[...], gf_ref[...])


def _proj_mlp(y, h, w_o, g, w_up, w_down, g_final, final_norm):
    m, d = h.shape
    d_ff = w_up.shape[1]
    row_spec = pl.BlockSpec((MLP_ROWS, d), lambda i: (i, 0))
    return pl.pallas_call(
        functools.partial(_mlp_kernel, final_norm=final_norm),
        grid=(m // MLP_ROWS,),
        in_specs=[row_spec, row_spec, _resident((d, d)), _resident((1, d)), _resident((d, d_ff)),
                  _resident((d_ff, d)), _resident((1, d))],
        out_specs=row_spec,
        out_shape=jax.ShapeDtypeStruct((m, d), _F32),
        scratch_shapes=[pltpu.VMEM((MLP_ROWS, d), _BF16)],
        compiler_params=pltpu.CompilerParams(dimension_semantics=("parallel",), vmem_limit_bytes=VMEM_LIMIT),
        name="proj_mlp",
    )(y, h, w_o, g, w_up, w_down, g_final)


def kernel(x, meta_tokens, norm_mix, norm_mlp, sb_w_qkv, sb_w_o, lru_w_in, lru_conv_w, lru_conv_b, lru_w_rg,
           lru_b_rg, lru_w_ig, lru_b_ig, lru_lambda, lru_w_out, mlp_w_up, mlp_w_down, norm_final):
    b, seq, d = x.shape
    assert d == HEADS * HEAD_DIM == LRU_BLOCKS * LRU_BLOCK_DIM
    t_len = N_META + seq
    tp = -(-t_len // TIME_BLOCK) * TIME_BLOCK
    assert (b * tp) % MLP_ROWS == 0 and tp % ATTN_BLOCK == 0

    meta = jnp.broadcast_to(meta_tokens[None].astype(x.dtype), (b, N_META, d))
    h = jnp.concatenate([meta, x, jnp.zeros((b, tp - t_len, d), x.dtype)], axis=1)

    row = lambda v: v.reshape(1, d)
    w_qkv = sb_w_qkv[0]
    q, kt, v = _qkv(h, row(norm_mix[0]), w_qkv[:, :d].astype(_BF16), w_qkv[:, d:2 * d].T.astype(_BF16),
                    w_qkv[:, 2 * d:].astype(_BF16))
    o = _sb_attention(q, kt, v)
    h = _proj_mlp(o.reshape(b * tp, d), h.reshape(b * tp, d), sb_w_o[0].astype(_BF16), row(norm_mlp[0]),
                  mlp_w_up[0].astype(_BF16), mlp_w_down[0].astype(_BF16), row(norm_final), False)

    w_gates = jnp.concatenate([lru_w_rg[0], lru_w_ig[0]], axis=-1).astype(_BF16)
    y = _lru_mixer(h.reshape(b, tp, d), row(norm_mix[1]), lru_w_in[0].astype(_BF16), lru_conv_w[0],
                   row(lru_conv_b[0]), w_gates, row(lru_b_rg[0]), row(lru_b_ig[0]), row(lru_lambda[0]))
    h = _proj_mlp(y.reshape(b * tp, d), h, lru_w_out[0].astype(_BF16), row(norm_mlp[1]),
                  mlp_w_up[1].astype(_BF16), mlp_w_down[1].astype(_BF16), row(norm_final), True)
    return h.reshape(b, tp, d)[:, N_META:t_len]
```

```python
import functools
import math

import jax
import jax.numpy as jnp
from jax import lax
from jax.experimental import pallas as pl
from jax.experimental.pallas import tpu as pltpu

N_META = 16
HEADS = 16
HEAD_DIM = 64
LRU_BLOCKS = 8
LRU_BLOCK_DIM = 128
CONV_WIDTH = 4
LRU_C = 8.0
EPS = 1e-6
LOG2E = 1.4426950408889634

LANES = 128
SUBLANES = 8
TIME_BLOCK = 384
ATTN_BLOCK = 128
ATTN_Q_ROWS = 512
ATTN_KEY_GROUP = 4
ATTN_SUFFIX_BLOCK = 256
ATTN_HEADS_PER_STEP = 4
MLP_ROWS = 512
MLP_FF_CHUNK = 1024
LRU_SCAN_GROUPS_PER_PIECE = 8
LRU_LAYER_FF_CHUNK = 512
VMEM_LIMIT = 56 * 1024 * 1024

_BF16 = jnp.bfloat16
_F32 = jnp.float32


def _dot(a, b):
    return jnp.dot(a, b, preferred_element_type=_F32)


def _rms_norm(x, g):
    ms = jnp.mean(x * x, axis=-1, keepdims=True)
    return (x * lax.rsqrt(ms + EPS)) * g


def _resident(shape):
    zeros = (0,) * len(shape)
    return pl.BlockSpec(shape, lambda *_: zeros, pipeline_mode=pl.Buffered(1))


def _qkv_kernel(h_ref, g_ref, wq_ref, wkt_ref, wv_ref, q_ref, kt_ref, v_ref):
    hn = _rms_norm(h_ref[...], g_ref[...]).astype(_BF16)
    q_ref[...] = (_dot(hn, wq_ref[...]) * (HEAD_DIM ** -0.5 * LOG2E)).astype(_BF16)
    v_ref[...] = _dot(hn, wv_ref[...]).astype(_BF16)
    kt = lax.dot_general(wkt_ref[...], hn, (((1,), (1,)), ((), ())), preferred_element_type=_F32)
    kt_ref[...] = kt.astype(_BF16)


def _qkv(h, g, wq, wkt, wv):
    b, tp, d = h.shape
    nt = tp // TIME_BLOCK
    row_spec = pl.BlockSpec((None, TIME_BLOCK, d), lambda i, j: (i, j, 0))
    return pl.pallas_call(
        _qkv_kernel,
        grid=(b, nt),
        in_specs=[row_spec, _resident((1, d)), _resident((d, d)), _resident((d, d)), _resident((d, d))],
        out_specs=[row_spec, pl.BlockSpec((None, d, TIME_BLOCK), lambda i, j: (i, 0, j)), row_spec],
        out_shape=[jax.ShapeDtypeStruct((b, tp, d), _BF16), jax.ShapeDtypeStruct((b, d, tp), _BF16),
                   jax.ShapeDtypeStruct((b, tp, d), _BF16)],
        compiler_params=pltpu.CompilerParams(dimension_semantics=("parallel", "parallel"),
                                             vmem_limit_bytes=VMEM_LIMIT),
        name="qkv",
    )(h, g, wq, wkt, wv)


def _attn_kernel(q_ref, kt_ref, v_ref, nu_ref, o_ref, acc_ref, carry_ref, z_ref, w_ref):
    blk, qrows, group = ATTN_BLOCK, ATTN_Q_ROWS, ATTN_KEY_GROUP
    lanes = q_ref.shape[1]
    n_heads = lanes // HEAD_DIM

    def head_of_lane(rows):
        return lax.broadcasted_iota(jnp.int32, (rows, lanes), 1) // HEAD_DIM

    def stacked_values(key_blk, keys):
        v = v_ref[pl.ds(pl.multiple_of(key_blk * blk, blk), keys), :]
        v_head = head_of_lane(keys)
        return jnp.concatenate([jnp.where(v_head == hd, v, jnp.zeros_like(v)) for hd in range(n_heads)], axis=0)

    def scores(q_head, key_blk, keys):
        return _dot(q_head, kt_ref[:, pl.ds(pl.multiple_of(key_blk * blk, blk), keys)])

    def visibility(rows, n_blk, causal_shift):
        masks = []
        for k_idx in range(n_blk):
            if causal_shift is None or (k_idx + 1) * blk - causal_shift <= 0:
                masks.append(None)
            else:
                key_pos = lax.broadcasted_iota(jnp.int32, (rows, blk), 1) + (k_idx * blk - causal_shift)
                masks.append(key_pos < lax.broadcasted_iota(jnp.int32, (rows, blk), 0))
        return masks

    def weights(z, hd, rows, widths, masks):
        keys = sum(widths)
        sp = jnp.maximum(jnp.log2(1.0 + jnp.exp2(jnp.minimum(z, 100.0))), z)
        sp = jnp.concatenate([sp[:, k * blk:(k + 1) * blk] if m is None else
                              jnp.where(m, sp[:, k * blk:(k + 1) * blk], 0.0)
                              for k, m in enumerate(masks)], axis=1).astype(_BF16)
        carry = carry_ref[hd, :rows, :]
        w_head = [None] * len(masks)
        lo = keys
        for width in reversed(widths):
            lo -= width
            suffix = _dot(sp[:, lo:lo + width], nu_ref[:width, :width])
            for c in range(width // blk):
                k_idx = lo // blk + c
                w = jnp.exp2(z[:, k_idx * blk:(k_idx + 1) * blk] + suffix[:, c * blk:(c + 1) * blk] + carry)
                if masks[k_idx] is not None:
                    w = jnp.where(masks[k_idx], w, 0.0)
                w_head[k_idx] = w.astype(_BF16)
            carry = carry + jnp.broadcast_to(suffix[:, 0:1], (rows, LANES))
        carry_ref[hd, :rows, :] = carry
        return w_head

    def sweep(qh, rows, key_blk, widths, causal_shift):
        keys = sum(widths)
        masks = visibility(rows, keys // blk, causal_shift)
        ws = []
        for hd in range(n_heads):
            ws += weights(scores(qh[hd], key_blk, keys), hd, rows, widths, masks)
        acc_ref[:rows, :] += _dot(jnp.concatenate(ws, axis=1), stacked_values(key_blk, keys))

    def start_block(row0, rows):
        q = q_ref[pl.ds(row0, rows), :]
        q_head = head_of_lane(rows)
        acc_ref[...] = jnp.zeros_like(acc_ref)
        carry_ref[...] = jnp.zeros_like(carry_ref)
        return [jnp.where(q_head == hd, q, jnp.zeros_like(q)) for hd in range(n_heads)]

    def write(row0, rows):
        o_ref[pl.ds(row0, rows), :] = acc_ref[:rows, :].astype(o_ref.dtype)

    sweep(start_block(0, blk), blk, 0, [blk], 0)
    write(0, blk)

    group_widths = [ATTN_SUFFIX_BLOCK] * (group * blk // ATTN_SUFFIX_BLOCK)
    group_keys = group * blk
    assert group_keys == qrows

    def wide_block(qi, _):
        row0 = pl.multiple_of(blk + qi * qrows, blk)
        first = 1 + qi * group
        qh = start_block(row0, qrows)

        def item_start(g):
            return jnp.maximum(first - (g + 1) * group, 0)

        def store_weights(w_head, hd):
            w_ref[:, hd * group_keys:(hd + 1) * group_keys] = jnp.concatenate(w_head, axis=1)

        def pv(g):
            acc_ref[...] += _dot(w_ref[...], stacked_values(item_start(g), group_keys))

        diag_masks = visibility(qrows, group, 0)
        for hd in range(n_heads):
            store_weights(weights(scores(qh[hd], first, group_keys), hd, qrows, group_widths, diag_masks), hd)
            z_ref[hd] = scores(qh[hd], item_start(0), group_keys)

        def key_group(g, _):
            pv(g - 1)
            for hd in range(n_heads):
                store_weights(weights(z_ref[hd], hd, qrows, group_widths, [None] * group), hd)
                z_ref[hd] = scores(qh[hd], item_start(g + 1), group_keys)
            return 0

        lax.fori_loop(0, qi, key_group, 0)
        pv(qi - 1)
        sweep(qh, qrows, 0, [blk], None)
        write(row0, qrows)
        return 0

    lax.fori_loop(0, (q_ref.shape[0] - blk) // qrows, wide_block, 0)


def _suffix_matrix():
    j = jnp.arange(ATTN_SUFFIX_BLOCK)[:, None]
    s = jnp.arange(ATTN_SUFFIX_BLOCK)[None, :]
    return -(j >= s).astype(_BF16)


def _sb_attention(q, kt, v):
    b, tp, d = q.shape
    assert (tp - ATTN_BLOCK) % ATTN_Q_ROWS == 0 and (ATTN_Q_ROWS // ATTN_BLOCK) % ATTN_KEY_GROUP == 0
    lanes = ATTN_HEADS_PER_STEP * HEAD_DIM
    return pl.pallas_call(
        _attn_kernel,
        grid=(b, d // lanes),
        in_specs=[pl.BlockSpec((None, tp, lanes), lambda i, j: (i, 0, j)),
                  pl.BlockSpec((None, lanes, tp), lambda i, j: (i, j, 0)),
                  pl.BlockSpec((None, tp, lanes), lambda i, j: (i, 0, j)),
                  _resident((ATTN_SUFFIX_BLOCK, ATTN_SUFFIX_BLOCK))],
        out_specs=pl.BlockSpec((None, tp, lanes), lambda i, j: (i, 0, j)),
        out_shape=jax.ShapeDtypeStruct((b, tp, d), _BF16),
        scratch_shapes=[pltpu.VMEM((ATTN_Q_ROWS, lanes), _F32),
                        pltpu.VMEM((ATTN_HEADS_PER_STEP, ATTN_Q_ROWS, LANES), _F32),
                        pltpu.VMEM((ATTN_HEADS_PER_STEP, ATTN_Q_ROWS, ATTN_KEY_GROUP * ATTN_BLOCK), _F32),
                        pltpu.VMEM((ATTN_Q_ROWS, ATTN_HEADS_PER_STEP * ATTN_KEY_GROUP * ATTN_BLOCK), _BF16)],
        compiler_params=pltpu.CompilerParams(dimension_semantics=("parallel", "parallel"),
                                             vmem_limit_bytes=VMEM_LIMIT),
        name="sb_attn",
    )(q, kt, v, _suffix_matrix())


def _gelu_tanh(x):
    return 0.5 * x * (1.0 + jnp.tanh(math.sqrt(2.0 / math.pi) * (x + 0.044715 * (x * x * x))))


def _sigmoid(x):
    return 1.0 / (1.0 + jnp.exp(-x))


def _lru_stage(h_ref, g_ref, win_ref, cw_ref, cb_ref, wg_ref, brg_ref, big_ref, lam_ref, y_ref,
               rec_ref, a_ref, b_ref, state_ref):
    rows, d = h_ref.shape
    tail = SUBLANES

    hn = _rms_norm(h_ref[...], g_ref[...]).astype(_BF16)
    gate = _gelu_tanh(_dot(hn, win_ref[:, :d]))
    rec_ref[tail:, :] = _dot(hn, win_ref[:, d:])
    yield

    u = cb_ref[...] + cw_ref[CONV_WIDTH - 1:CONV_WIDTH, :] * rec_ref[tail:, :]
    for j in range(CONV_WIDTH - 1):
        shift = CONV_WIDTH - 1 - j
        u = u + cw_ref[j:j + 1, :] * rec_ref[tail - shift:tail - shift + rows, :]
    rec_ref[0:tail, :] = rec_ref[rows:rows + tail, :]
    yield

    neg_lam = -lam_ref[...]
    log_a_unit = -LRU_C * (jnp.maximum(neg_lam, 0.0) + jnp.log1p(jnp.exp(-jnp.abs(neg_lam))))
    ub = u.astype(_BF16)
    for n in range(LRU_BLOCKS):
        cols = slice(n * LRU_BLOCK_DIM, (n + 1) * LRU_BLOCK_DIM)
        ri = _dot(ub[:, cols], wg_ref[n])
        r = _sigmoid(ri[:, :LRU_BLOCK_DIM] + brg_ref[:, cols])
        i = _sigmoid(ri[:, LRU_BLOCK_DIM:] + big_ref[:, cols])
        log_a = log_a_unit[:, cols] * r
        a = jnp.exp(log_a)
        one_minus_a2 = -jnp.tanh(log_a) * (1.0 + a * a)
        mult = jnp.where(one_minus_a2 > 0.0, one_minus_a2 * lax.rsqrt(one_minus_a2), 0.0)
        a_ref[:, cols] = a
        b_ref[:, cols] = mult * (i * u[:, cols])
        yield

    sub = lax.broadcasted_iota(jnp.int32, (SUBLANES, d), 0)
    state = state_ref[...]
    for gidx in range(rows // SUBLANES):
        r0 = gidx * SUBLANES
        av = a_ref[r0:r0 + SUBLANES, :]
        bv = b_ref[r0:r0 + SUBLANES, :]
        for s in (1, 2, 4):
            keep = sub >= s
            bv = jnp.where(keep, av * pltpu.roll(bv, s, axis=0) + bv, bv)
            av = jnp.where(keep, av * pltpu.roll(av, s, axis=0), av)
        hv = av * state + bv
        b_ref[r0:r0 + SUBLANES, :] = hv
        state = jnp.broadcast_to(hv[SUBLANES - 1:SUBLANES, :], (SUBLANES, d))
        if gidx % LRU_SCAN_GROUPS_PER_PIECE == LRU_SCAN_GROUPS_PER_PIECE - 1:
            yield
    state_ref[...] = state
    y_ref[...] = (b_ref[...] * gate).astype(y_ref.dtype)


def _mlp_stage(y, h, wo_ref, g_ref, wup_ref, wdn_ref, gf_ref, o_ref, hn_ref, final_norm, ff_chunk):
    d_ff = wup_ref.shape[1]
    h1 = h + _dot(y, wo_ref[...])
    hn_ref[...] = _rms_norm(h1, g_ref[...]).astype(_BF16)
    o_ref[...] = h1
    yield
    for c in range(0, d_ff, ff_chunk):
        up = jnp.maximum(_dot(hn_ref[...], wup_ref[:, c:c + ff_chunk]), 0.0)
        up = (up * up).astype(_BF16)
        yield
        o_ref[...] += _dot(up, wdn_ref[c:c + ff_chunk, :])
        yield
    if final_norm:
        o_ref[...] = _rms_norm(o_ref[...], gf_ref[...])


def _interleave(*stages):
    live = list(stages)
    while live:
        for stage in list(live):
            if next(stage, StopIteration) is StopIteration:
                live.remove(stage)


def _mlp_kernel(y_ref, h_ref, wo_ref, g_ref, wup_ref, wdn_ref, gf_ref, o_ref, hn_ref, *, final_norm):
    _interleave(_mlp_stage(y_ref[...], h_ref[...], wo_ref, g_ref, wup_ref, wdn_ref, gf_ref, o_ref, hn_ref,
                           final_norm, MLP_FF_CHUNK))


def _proj_mlp(y, h, w_o, g, w_up, w_down, g_final, final_norm):
    m, d = h.shape
    d_ff = w_up.shape[1]
    row_spec = pl.BlockSpec((MLP_ROWS, d), lambda i: (i, 0))
    return pl.pallas_call(
        functools.partial(_mlp_kernel, final_norm=final_norm),
        grid=(m // MLP_ROWS,),
        in_specs=[row_spec, row_spec, _resident((d, d)), _resident((1, d)), _resident((d, d_ff)),
                  _resident((d_ff, d)), _resident((1, d))],
        out_specs=row_spec,
        out_shape=jax.ShapeDtypeStruct((m, d), _F32),
        scratch_shapes=[pltpu.VMEM((MLP_ROWS, d), _BF16)],
        compiler_params=pltpu.CompilerParams(dimension_semantics=("parallel",), vmem_limit_bytes=VMEM_LIMIT),
        name="proj_mlp",
    )(y, h, w_o, g, w_up, w_down, g_final)


def _lru_layer_kernel(hcur_ref, hprev_ref, g_ref, win_ref, cw_ref, cb_ref, wg_ref, brg_ref, big_ref, lam_ref,
                      wo_ref, gm_ref, wup_ref, wdn_ref, gf_ref, o_ref,
                      rec_ref, a_ref, b_ref, state_ref, y_ref, hn_ref, *, steps_per_batch):
    s = pl.program_id(0)

    @pl.when(s == 0)
    def _():
        y_ref[...] = jnp.zeros_like(y_ref)

    @pl.when(s % steps_per_batch == 0)
    def _():
        rec_ref[0:SUBLANES, :] = jnp.zeros((SUBLANES, rec_ref.shape[1]), _F32)
        state_ref[...] = jnp.zeros_like(state_ref)

    y_prev = y_ref[...]
    _interleave(
        _lru_stage(hcur_ref, g_ref, win_ref, cw_ref, cb_ref, wg_ref, brg_ref, big_ref, lam_ref, y_ref,
                   rec_ref, a_ref, b_ref, state_ref),
        _mlp_stage(y_prev, hprev_ref[...], wo_ref, gm_ref, wup_ref, wdn_ref, gf_ref, o_ref, hn_ref, True,
                   LRU_LAYER_FF_CHUNK))


def _lru_layer(h, steps_per_batch, g, w_in, conv_w, conv_b, w_gates, b_rg, b_ig, lam, w_out, g_mlp, w_up, w_down,
               g_final):
    m, d = h.shape
    d_ff = w_up.shape[1]
    n_blocks = m // TIME_BLOCK
    cur_spec = pl.BlockSpec((TIME_BLOCK, d), lambda s: (jnp.minimum(s, n_blocks - 1), 0))
    prev_spec = pl.BlockSpec((TIME_BLOCK, d), lambda s: (jnp.maximum(s - 1, 0), 0))
    return pl.pallas_call(
        functools.partial(_lru_layer_kernel, steps_per_batch=steps_per_batch),
        grid=(n_blocks + 1,),
        in_specs=[cur_spec, prev_spec, _resident((1, d)), _resident((d, 2 * d)), _resident((CONV_WIDTH, d)),
                  _resident((1, d)), _resident((LRU_BLOCKS, LRU_BLOCK_DIM, 2 * LRU_BLOCK_DIM)),
                  _resident((1, d)), _resident((1, d)), _resident((1, d)),
                  _resident((d, d)), _resident((1, d)), _resident((d, d_ff)), _resident((d_ff, d)),
                  _resident((1, d))],
        out_specs=prev_spec,
        out_shape=jax.ShapeDtypeStruct((m, d), _F32),
        scratch_shapes=[pltpu.VMEM((SUBLANES + TIME_BLOCK, d), _F32), pltpu.VMEM((TIME_BLOCK, d), _F32),
                        pltpu.VMEM((TIME_BLOCK, d), _F32), pltpu.VMEM((SUBLANES, d), _F32),
                        pltpu.VMEM((TIME_BLOCK, d), _BF16), pltpu.VMEM((TIME_BLOCK, d), _BF16)],
        compiler_params=pltpu.CompilerParams(dimension_semantics=("arbitrary",), vmem_limit_bytes=VMEM_LIMIT),
        name="lru_layer",
    )(h, h, g, w_in, conv_w, conv_b, w_gates, b_rg, b_ig, lam, w_out, g_mlp, w_up, w_down, g_final)


def kernel(x, meta_tokens, norm_mix, norm_mlp, sb_w_qkv, sb_w_o, lru_w_in, lru_conv_w, lru_conv_b, lru_w_rg,
           lru_b_rg, lru_w_ig, lru_b_ig, lru_lambda, lru_w_out, mlp_w_up, mlp_w_down, norm_final):
    b, seq, d = x.shape
    assert d == HEADS * HEAD_DIM == LRU_BLOCKS * LRU_BLOCK_DIM
    t_len = N_META + seq
    tp = -(-t_len // TIME_BLOCK) * TIME_BLOCK
    assert (b * tp) % MLP_ROWS == 0 and tp % ATTN_BLOCK == 0

    meta = jnp.broadcast_to(meta_tokens[None].astype(x.dtype), (b, N_META, d))
    h = jnp.concatenate([meta, x, jnp.zeros((b, tp - t_len, d), x.dtype)], axis=1)

    row = lambda v: v.reshape(1, d)
    w_qkv = sb_w_qkv[0]
    q, kt, v = _qkv(h, row(norm_mix[0]), w_qkv[:, :d].astype(_BF16), w_qkv[:, d:2 * d].T.astype(_BF16),
                    w_qkv[:, 2 * d:].astype(_BF16))
    o = _sb_attention(q, kt, v)
    h = _proj_mlp(o.reshape(b * tp, d), h.reshape(b * tp, d), sb_w_o[0].astype(_BF16), row(norm_mlp[0]),
                  mlp_w_up[0].astype(_BF16), mlp_w_down[0].astype(_BF16), row(norm_final), False)

    w_gates = jnp.concatenate([lru_w_rg[0], lru_w_ig[0]], axis=-1).astype(_BF16)
    h = _lru_layer(h, tp // TIME_BLOCK, row(norm_mix[1]), lru_w_in[0].astype(_BF16), lru_conv_w[0],
                   row(lru_conv_b[0]), w_gates, row(lru_b_rg[0]), row(lru_b_ig[0]), row(lru_lambda[0]),
                   lru_w_out[0].astype(_BF16), row(norm_mlp[1]), mlp_w_up[1].astype(_BF16),
                   mlp_w_down[1].astype(_BF16), row(norm_final))
    return h.reshape(b, tp, d)[:, N_META:t_len]
```

```python
import functools
import math

import jax
import jax.numpy as jnp
from jax import lax
from jax.experimental import pallas as pl
from jax.experimental.pallas import tpu as pltpu

N_META = 16
HEADS = 16
HEAD_DIM = 64
LRU_BLOCKS = 8
LRU_BLOCK_DIM = 128
CONV_WIDTH = 4
LRU_C = 8.0
EPS = 1e-6
LOG2E = 1.4426950408889634

LANES = 128
SUBLANES = 8
TIME_BLOCK = 384
ATTN_BLOCK = 128
ATTN_Q_ROWS = 512
ATTN_KEY_GROUP = 4
ATTN_SUFFIX_BLOCK = 256
ATTN_HEADS_PER_STEP = 4
MLP_ROWS = 512
MLP_FF_CHUNK = 1024
LRU_SCAN_GROUPS_PER_PIECE = 8
LRU_LAYER_FF_CHUNK = 512
VMEM_LIMIT = 56 * 1024 * 1024

_BF16 = jnp.bfloat16
_F32 = jnp.float32


def _dot(a, b):
    return jnp.dot(a, b, preferred_element_type=_F32)


def _rms_norm(x, g):
    ms = jnp.mean(x * x, axis=-1, keepdims=True)
    return (x * lax.rsqrt(ms + EPS)) * g


def _resident(shape):
    zeros = (0,) * len(shape)
    return pl.BlockSpec(shape, lambda *_: zeros, pipeline_mode=pl.Buffered(1))


def _qkv_kernel(x_ref, meta_ref, g_ref, wq_ref, wkt_ref, wv_ref, q_ref, kt_ref, v_ref, h_ref, *, pad_rows):
    j, last = pl.program_id(1), pl.num_programs(1) - 1
    rows, d = x_ref.shape
    x = x_ref[...]
    first_block = jnp.concatenate([meta_ref[...], x[:rows - N_META]], axis=0)
    last_block = jnp.concatenate([x[pad_rows:], jnp.zeros((pad_rows, d), x.dtype)], axis=0)
    h = jnp.where(j == 0, first_block, jnp.where(j == last, last_block, x))
    h_ref[...] = h
    hn = _rms_norm(h, g_ref[...]).astype(_BF16)
    q_ref[...] = (_dot(hn, wq_ref[...]) * (HEAD_DIM ** -0.5 * LOG2E)).astype(_BF16)
    v_ref[...] = _dot(hn, wv_ref[...]).astype(_BF16)
    kt = lax.dot_general(wkt_ref[...], hn, (((1,), (1,)), ((), ())), preferred_element_type=_F32)
    kt_ref[...] = kt.astype(_BF16)


def _qkv(x, meta, tp, g, wq, wkt, wv):
    b, seq, d = x.shape
    nt = tp // TIME_BLOCK
    pad_rows = tp - N_META - seq
    assert nt >= 2 and seq >= TIME_BLOCK and all(n % SUBLANES == 0 for n in (pad_rows, N_META, seq, TIME_BLOCK))
    row_spec = pl.BlockSpec((None, TIME_BLOCK, d), lambda i, j: (i, j, 0))
    tiles, meta_tiles = TIME_BLOCK // SUBLANES, N_META // SUBLANES
    x_spec = pl.BlockSpec(
        (pl.Squeezed(), pl.Element(TIME_BLOCK), pl.Element(d)),
        lambda i, j: (i, jnp.clip(j * tiles - meta_tiles, 0, (seq - TIME_BLOCK) // SUBLANES) * SUBLANES, 0))
    return pl.pallas_call(
        functools.partial(_qkv_kernel, pad_rows=pad_rows),
        grid=(b, nt),
        in_specs=[x_spec, _resident((N_META, d)), _resident((1, d)), _resident((d, d)), _resident((d, d)),
                  _resident((d, d))],
        out_specs=[row_spec, pl.BlockSpec((None, d, TIME_BLOCK), lambda i, j: (i, 0, j)), row_spec, row_spec],
        out_shape=[jax.ShapeDtypeStruct((b, tp, d), _BF16), jax.ShapeDtypeStruct((b, d, tp), _BF16),
                   jax.ShapeDtypeStruct((b, tp, d), _BF16), jax.ShapeDtypeStruct((b, tp, d), _F32)],
        compiler_params=pltpu.CompilerParams(dimension_semantics=("parallel", "parallel"),
                                             vmem_limit_bytes=VMEM_LIMIT),
        name="qkv",
    )(x, meta, g, wq, wkt, wv)


def _attn_kernel(q_ref, kt_ref, v_ref, nu_ref, o_ref, acc_ref, carry_ref, z_ref, w_ref):
    blk, qrows, group = ATTN_BLOCK, ATTN_Q_ROWS, ATTN_KEY_GROUP
    lanes = q_ref.shape[1]
    n_heads = lanes // HEAD_DIM

    def head_of_lane(rows):
        return lax.broadcasted_iota(jnp.int32, (rows, lanes), 1) // HEAD_DIM

    def stacked_values(key_blk, keys):
        v = v_ref[pl.ds(pl.multiple_of(key_blk * blk, blk), keys), :]
        v_head = head_of_lane(keys)
        return jnp.concatenate([jnp.where(v_head == hd, v, jnp.zeros_like(v)) for hd in range(n_heads)], axis=0)

    def scores(q_head, key_blk, keys):
        return _dot(q_head, kt_ref[:, pl.ds(pl.multiple_of(key_blk * blk, blk), keys)])

    def visibility(rows, n_blk, causal_shift):
        masks = []
        for k_idx in range(n_blk):
            if causal_shift is None or (k_idx + 1) * blk - causal_shift <= 0:
                masks.append(None)
            else:
                key_pos = lax.broadcasted_iota(jnp.int32, (rows, blk), 1) + (k_idx * blk - causal_shift)
                masks.append(key_pos < lax.broadcasted_iota(jnp.int32, (rows, blk), 0))
        return masks

    def softplus_keys(z, masks):
        sp = jnp.maximum(jnp.log2(1.0 + jnp.exp2(jnp.minimum(z, 100.0))), z)
        return jnp.concatenate([sp[:, k * blk:(k + 1) * blk] if m is None else
                                jnp.where(m, sp[:, k * blk:(k + 1) * blk], 0.0)
                                for k, m in enumerate(masks)], axis=1).astype(_BF16)

    def suffix_sums(sp, widths):
        out, lo = [], 0
        for width in widths:
            out.append(_dot(sp[:, lo:lo + width], nu_ref[:width, :width]))
            lo += width
        return out

    def weights_from(z, suffixes, hd, rows, widths, masks):
        carry = carry_ref[hd, :rows, :]
        w_head = [None] * len(masks)
        lo = sum(widths)
        for width, suffix in zip(reversed(widths), reversed(suffixes)):
            lo -= width
            for c in range(width // blk):
                k_idx = lo // blk + c
                w = jnp.exp2(z[:, k_idx * blk:(k_idx + 1) * blk] + suffix[:, c * blk:(c + 1) * blk] + carry)
                if masks[k_idx] is not None:
                    w = jnp.where(masks[k_idx], w, 0.0)
                w_head[k_idx] = w.astype(_BF16)
            carry = carry + jnp.broadcast_to(suffix[:, 0:1], (rows, LANES))
        carry_ref[hd, :rows, :] = carry
        return w_head

    def weights(z, hd, rows, widths, masks):
        return weights_from(z, suffix_sums(softplus_keys(z, masks), widths), hd, rows, widths, masks)

    def sweep(qh, rows, key_blk, widths, causal_shift, between=None):
        keys = sum(widths)
        masks = visibility(rows, keys // blk, causal_shift)
        zs = [scores(qh[hd], key_blk, keys) for hd in range(n_heads)]
        if between is not None:
            between()
        suffixes = [suffix_sums(softplus_keys(z, masks), widths) for z in zs]
        ws = []
        for hd in range(n_heads):
            ws += weights_from(zs[hd], suffixes[hd], hd, rows, widths, masks)
        acc_ref[:rows, :] += _dot(jnp.concatenate(ws, axis=1), stacked_values(key_blk, keys))

    def start_block(row0, rows):
        q = q_ref[pl.ds(row0, rows), :]
        q_head = head_of_lane(rows)
        acc_ref[...] = jnp.zeros_like(acc_ref)
        carry_ref[...] = jnp.zeros_like(carry_ref)
        return [jnp.where(q_head == hd, q, jnp.zeros_like(q)) for hd in range(n_heads)]

    def write(row0, rows):
        o_ref[pl.ds(row0, rows), :] = acc_ref[:rows, :].astype(o_ref.dtype)

    sweep(start_block(0, blk), blk, 0, [blk], 0)
    write(0, blk)

    group_widths = [ATTN_SUFFIX_BLOCK] * (group * blk // ATTN_SUFFIX_BLOCK)
    group_keys = group * blk
    assert group_keys == qrows

    def wide_block(qi, _):
        row0 = pl.multiple_of(blk + qi * qrows, blk)
        first = 1 + qi * group
        qh = start_block(row0, qrows)

        def item_start(g):
            return jnp.maximum(first - (g + 1) * group, 0)

        def store_weights(w_head, hd):
            w_ref[:, hd * group_keys:(hd + 1) * group_keys] = jnp.concatenate(w_head, axis=1)

        def pv(g):
            acc_ref[...] += _dot(w_ref[...], stacked_values(item_start(g), group_keys))

        diag_masks = visibility(qrows, group, 0)
        for hd in range(n_heads):
            store_weights(weights(scores(qh[hd], first, group_keys), hd, qrows, group_widths, diag_masks), hd)
            z_ref[hd] = scores(qh[hd], item_start(0), group_keys)

        def key_group(g, _):
            pv(g - 1)
            no_mask = [None] * group
            suffixes = suffix_sums(softplus_keys(z_ref[0], no_mask), group_widths)
            for hd in range(n_heads):
                ahead = (suffix_sums(softplus_keys(z_ref[hd + 1], no_mask), group_widths)
                         if hd + 1 < n_heads else None)
                store_weights(weights_from(z_ref[hd], suffixes, hd, qrows, group_widths, no_mask), hd)
                z_ref[hd] = scores(qh[hd], item_start(g + 1), group_keys)
                suffixes = ahead
            return 0

        lax.fori_loop(0, qi, key_group, 0)
        sweep(qh, qrows, 0, [blk], None, between=lambda: pv(qi - 1))
        write(row0, qrows)
        return 0

    lax.fori_loop(0, (q_ref.shape[0] - blk) // qrows, wide_block, 0)


def _suffix_matrix():
    j = jnp.arange(ATTN_SUFFIX_BLOCK)[:, None]
    s = jnp.arange(ATTN_SUFFIX_BLOCK)[None, :]
    return -(j >= s).astype(_BF16)


def _sb_attention(q, kt, v):
    b, tp, d = q.shape
    assert (tp - ATTN_BLOCK) % ATTN_Q_ROWS == 0 and (ATTN_Q_ROWS // ATTN_BLOCK) % ATTN_KEY_GROUP == 0
    lanes = ATTN_HEADS_PER_STEP * HEAD_DIM
    return pl.pallas_call(
        _attn_kernel,
        grid=(b, d // lanes),
        in_specs=[pl.BlockSpec((None, tp, lanes), lambda i, j: (i, 0, j)),
                  pl.BlockSpec((None, lanes, tp), lambda i, j: (i, j, 0)),
                  pl.BlockSpec((None, tp, lanes), lambda i, j: (i, 0, j)),
                  _resident((ATTN_SUFFIX_BLOCK, ATTN_SUFFIX_BLOCK))],
        out_specs=pl.BlockSpec((None, tp, lanes), lambda i, j: (i, 0, j)),
        out_shape=jax.ShapeDtypeStruct((b, tp, d), _BF16),
        scratch_shapes=[pltpu.VMEM((ATTN_Q_ROWS, lanes), _F32),
                        pltpu.VMEM((ATTN_HEADS_PER_STEP, ATTN_Q_ROWS, LANES), _F32),
                        pltpu.VMEM((ATTN_HEADS_PER_STEP, ATTN_Q_ROWS, ATTN_KEY_GROUP * ATTN_BLOCK), _F32),
                        pltpu.VMEM((ATTN_Q_ROWS, ATTN_HEADS_PER_STEP * ATTN_KEY_GROUP * ATTN_BLOCK), _BF16)],
        compiler_params=pltpu.CompilerParams(dimension_semantics=("parallel", "parallel"),
                                             vmem_limit_bytes=VMEM_LIMIT),
        name="sb_attn",
    )(q, kt, v, _suffix_matrix())


def _gelu_tanh(x):
    return 0.5 * x * (1.0 + jnp.tanh(math.sqrt(2.0 / math.pi) * (x + 0.044715 * (x * x * x))))


def _sigmoid(x):
    return 1.0 / (1.0 + jnp.exp(-x))


def _lru_stage(h_ref, g_ref, win_ref, cw_ref, cb_ref, wg_ref, brg_ref, big_ref, lam_ref, y_ref,
               rec_ref, a_ref, b_ref, state_ref):
    rows, d = h_ref.shape
    tail = SUBLANES

    hn = _rms_norm(h_ref[...], g_ref[...]).astype(_BF16)
    gate = _gelu_tanh(_dot(hn, win_ref[:, :d]))
    rec_ref[tail:, :] = _dot(hn, win_ref[:, d:])
    yield

    u = cb_ref[...] + cw_ref[CONV_WIDTH - 1:CONV_WIDTH, :] * rec_ref[tail:, :]
    for j in range(CONV_WIDTH - 1):
        shift = CONV_WIDTH - 1 - j
        u = u + cw_ref[j:j + 1, :] * rec_ref[tail - shift:tail - shift + rows, :]
    rec_ref[0:tail, :] = rec_ref[rows:rows + tail, :]
    yield

    neg_lam = -lam_ref[...]
    log_a_unit = -LRU_C * (jnp.maximum(neg_lam, 0.0) + jnp.log1p(jnp.exp(-jnp.abs(neg_lam))))
    ub = u.astype(_BF16)
    for n in range(LRU_BLOCKS):
        cols = slice(n * LRU_BLOCK_DIM, (n + 1) * LRU_BLOCK_DIM)
        ri = _dot(ub[:, cols], wg_ref[n])
        r = _sigmoid(ri[:, :LRU_BLOCK_DIM] + brg_ref[:, cols])
        i = _sigmoid(ri[:, LRU_BLOCK_DIM:] + big_ref[:, cols])
        log_a = log_a_unit[:, cols] * r
        a = jnp.exp(log_a)
        one_minus_a2 = -jnp.tanh(log_a) * (1.0 + a * a)
        mult = jnp.where(one_minus_a2 > 0.0, one_minus_a2 * lax.rsqrt(one_minus_a2), 0.0)
        a_ref[:, cols] = a
        b_ref[:, cols] = mult * (i * u[:, cols])
        yield

    sub = lax.broadcasted_iota(jnp.int32, (SUBLANES, d), 0)
    state = state_ref[...]
    for gidx in range(rows // SUBLANES):
        r0 = gidx * SUBLANES
        av = a_ref[r0:r0 + SUBLANES, :]
        bv = b_ref[r0:r0 + SUBLANES, :]
        for s in (1, 2, 4):
            keep = sub >= s
            bv = jnp.where(keep, av * pltpu.roll(bv, s, axis=0) + bv, bv)
            av = jnp.where(keep, av * pltpu.roll(av, s, axis=0), av)
        hv = av * state + bv
        b_ref[r0:r0 + SUBLANES, :] = hv
        state = jnp.broadcast_to(hv[SUBLANES - 1:SUBLANES, :], (SUBLANES, d))
        if gidx % LRU_SCAN_GROUPS_PER_PIECE == LRU_SCAN_GROUPS_PER_PIECE - 1:
            yield
    state_ref[...] = state
    y_ref[...] = (b_ref[...] * gate).astype(y_ref.dtype)


def _mlp_stage(y, h, wo_ref, g_ref, wup_ref, wdn_ref, gf_ref, o_ref, hn_ref, final_norm, ff_chunk):
    d_ff = wup_ref.shape[1]
    h1 = h + _dot(y, wo_ref[...])
    hn_ref[...] = _rms_norm(h1, g_ref[...]).astype(_BF16)
    o_ref[...] = h1
    yield
    for c in range(0, d_ff, ff_chunk):
        up = jnp.maximum(_dot(hn_ref[...], wup_ref[:, c:c + ff_chunk]), 0.0)
        up = (up * up).astype(_BF16)
        yield
        o_ref[...] += _dot(up, wdn_ref[c:c + ff_chunk, :])
        yield
    if final_norm:
        o_ref[...] = _rms_norm(o_ref[...], gf_ref[...])


def _interleave(*stages):
    live = list(stages)
    while live:
        for stage in list(live):
            if next(stage, StopIteration) is StopIteration:
                live.remove(stage)


def _mlp_kernel(y_ref, h_ref, wo_ref, g_ref, wup_ref, wdn_ref, gf_ref, o_ref, hn_ref, *, final_norm):
    _interleave(_mlp_stage(y_ref[...], h_ref[...], wo_ref, g_ref, wup_ref, wdn_ref, gf_ref, o_ref, hn_ref,
                           final_norm, MLP_FF_CHUNK))


def _proj_mlp(y, h, w_o, g, w_up, w_down, g_final, final_norm):
    m, d = h.shape
    d_ff = w_up.shape[1]
    row_spec = pl.BlockSpec((MLP_ROWS, d), lambda i: (i, 0))
    return pl.pallas_call(
        functools.partial(_mlp_kernel, final_norm=final_norm),
        grid=(m // MLP_ROWS,),
        in_specs=[row_spec, row_spec, _resident((d, d)), _resident((1, d)), _resident((d, d_ff)),
                  _resident((d_ff, d)), _resident((1, d))],
        out_specs=row_spec,
        out_shape=jax.ShapeDtypeStruct((m, d), _F32),
        scratch_shapes=[pltpu.VMEM((MLP_ROWS, d), _BF16)],
        compiler_params=pltpu.CompilerParams(dimension_semantics=("parallel",), vmem_limit_bytes=VMEM_LIMIT),
        name="proj_mlp",
    )(y, h, w_o, g, w_up, w_down, g_final)


def _lru_layer_kernel(hcur_ref, hprev_ref, g_ref, win_ref, cw_ref, cb_ref, wg_ref, brg_ref, big_ref, lam_ref,
                      wo_ref, gm_ref, wup_ref, wdn_ref, gf_ref, o_ref,
                      rec_ref, a_ref, b_ref, state_ref, y_ref, hn_ref, *, steps_per_batch):
    s = pl.program_id(0)

    @pl.when(s == 0)
    def _():
        y_ref[...] = jnp.zeros_like(y_ref)

    @pl.when(s % steps_per_batch == 0)
    def _():
        rec_ref[0:SUBLANES, :] = jnp.zeros((SUBLANES, rec_ref.shape[1]), _F32)
        state_ref[...] = jnp.zeros_like(state_ref)

    y_prev = y_ref[...]
    _interleave(
        _lru_stage(hcur_ref, g_ref, win_ref, cw_ref, cb_ref, wg_ref, brg_ref, big_ref, lam_ref, y_ref,
                   rec_ref, a_ref, b_ref, state_ref),
        _mlp_stage(y_prev, hprev_ref[...], wo_ref, gm_ref, wup_ref, wdn_ref, gf_ref, o_ref, hn_ref, True,
                   LRU_LAYER_FF_CHUNK))


def _lru_layer(h, steps_per_batch, g, w_in, conv_w, conv_b, w_gates, b_rg, b_ig, lam, w_out, g_mlp, w_up, w_down,
               g_final):
    m, d = h.shape
    d_ff = w_up.shape[1]
    n_blocks = m // TIME_BLOCK
    cur_spec = pl.BlockSpec((TIME_BLOCK, d), lambda s: (jnp.minimum(s, n_blocks - 1), 0))
    prev_spec = pl.BlockSpec((TIME_BLOCK, d), lambda s: (jnp.maximum(s - 1, 0), 0))
    return pl.pallas_call(
        functools.partial(_lru_layer_kernel, steps_per_batch=steps_per_batch),
        grid=(n_blocks + 1,),
        in_specs=[cur_spec, prev_spec, _resident((1, d)), _resident((d, 2 * d)), _resident((CONV_WIDTH, d)),
                  _resident((1, d)), _resident((LRU_BLOCKS, LRU_BLOCK_DIM, 2 * LRU_BLOCK_DIM)),
                  _resident((1, d)), _resident((1, d)), _resident((1, d)),
                  _resident((d, d)), _resident((1, d)), _resident((d, d_ff)), _resident((d_ff, d)),
                  _resident((1, d))],
        out_specs=prev_spec,
        out_shape=jax.ShapeDtypeStruct((m, d), _F32),
        scratch_shapes=[pltpu.VMEM((SUBLANES + TIME_BLOCK, d), _F32), pltpu.VMEM((TIME_BLOCK, d), _F32),
                        pltpu.VMEM((TIME_BLOCK, d), _F32), pltpu.VMEM((SUBLANES, d), _F32),
                        pltpu.VMEM((TIME_BLOCK, d), _BF16), pltpu.VMEM((TIME_BLOCK, d), _BF16)],
        compiler_params=pltpu.CompilerParams(dimension_semantics=("arbitrary",), vmem_limit_bytes=VMEM_LIMIT),
        name="lru_layer",
    )(h, h, g, w_in, conv_w, conv_b, w_gates, b_rg, b_ig, lam, w_out, g_mlp, w_up, w_down, g_final)


def kernel(x, meta_tokens, norm_mix, norm_mlp, sb_w_qkv, sb_w_o, lru_w_in, lru_conv_w, lru_conv_b, lru_w_rg,
           lru_b_rg, lru_w_ig, lru_b_ig, lru_lambda, lru_w_out, mlp_w_up, mlp_w_down, norm_final):
    b, seq, d = x.shape
    assert d == HEADS * HEAD_DIM == LRU_BLOCKS * LRU_BLOCK_DIM
    t_len = N_META + seq
    tp = -(-t_len // TIME_BLOCK) * TIME_BLOCK
    assert (b * tp) % MLP_ROWS == 0 and tp % ATTN_BLOCK == 0

    row = lambda v: v.reshape(1, d)
    w_qkv = sb_w_qkv[0]
    q, kt, v, h = _qkv(x, meta_tokens.astype(x.dtype), tp, row(norm_mix[0]), w_qkv[:, :d].astype(_BF16),
                       w_qkv[:, d:2 * d].T.astype(_BF16), w_qkv[:, 2 * d:].astype(_BF16))
    o = _sb_attention(q, kt, v)
    h = _proj_mlp(o.reshape(b * tp, d), h.reshape(b * tp, d), sb_w_o[0].astype(_BF16), row(norm_mlp[0]),
                  mlp_w_up[0].astype(_BF16), mlp_w_down[0].astype(_BF16), row(norm_final), False)

    w_gates = jnp.concatenate([lru_w_rg[0], lru_w_ig[0]], axis=-1).astype(_BF16)
    h = _lru_layer(h, tp // TIME_BLOCK, row(norm_mix[1]), lru_w_in[0].astype(_BF16), lru_conv_w[0],
                   row(lru_conv_b[0]), w_gates, row(lru_b_rg[0]), row(lru_b_ig[0]), row(lru_lambda[0]),
                   lru_w_out[0].astype(_BF16), row(norm_mlp[1]), mlp_w_up[1].astype(_BF16),
                   mlp_w_down[1].astype(_BF16), row(norm_final))
    return h.reshape(b, tp, d)[:, N_META:t_len]
```

```python
import functools
import math

import jax
import jax.numpy as jnp
from jax import lax
from jax.experimental import pallas as pl
from jax.experimental.pallas import tpu as pltpu

N_META = 16
HEADS = 16
HEAD_DIM = 64
LRU_BLOCKS = 8
LRU_BLOCK_DIM = 128
CONV_WIDTH = 4
LRU_C = 8.0
EPS = 1e-6
LOG2E = 1.4426950408889634

LANES = 128
SUBLANES = 8
TIME_BLOCK = 384
ATTN_BLOCK = 128
ATTN_Q_ROWS = 512
ATTN_KEY_GROUP = 4
ATTN_SUFFIX_BLOCK = 256
ATTN_HEADS_PER_STEP = 4
ATTN_LOG2_WEIGHT_FLOOR = -256.0
MLP_ROWS = 512
MLP_FF_CHUNK = 1024
LRU_SCAN_GROUPS_PER_PIECE = 8
LRU_LAYER_FF_CHUNK = 512
VMEM_LIMIT = 56 * 1024 * 1024

_BF16 = jnp.bfloat16
_F32 = jnp.float32


def _dot(a, b):
    return jnp.dot(a, b, preferred_element_type=_F32)


def _rms_norm(x, g):
    ms = jnp.mean(x * x, axis=-1, keepdims=True)
    return (x * lax.rsqrt(ms + EPS)) * g


def _resident(shape):
    zeros = (0,) * len(shape)
    return pl.BlockSpec(shape, lambda *_: zeros, pipeline_mode=pl.Buffered(1))


def _qkv_kernel(x_ref, meta_ref, g_ref, wq_ref, wkt_ref, wv_ref, q_ref, kt_ref, v_ref, h_ref, *, pad_rows):
    j, last = pl.program_id(1), pl.num_programs(1) - 1
    rows, d = x_ref.shape
    x = x_ref[...]
    first_block = jnp.concatenate([meta_ref[...], x[:rows - N_META]], axis=0)
    last_block = jnp.concatenate([x[pad_rows:], jnp.zeros((pad_rows, d), x.dtype)], axis=0)
    h = jnp.where(j == 0, first_block, jnp.where(j == last, last_block, x))
    h_ref[...] = h
    hn = _rms_norm(h, g_ref[...]).astype(_BF16)
    q_ref[...] = (_dot(hn, wq_ref[...]) * (HEAD_DIM ** -0.5 * LOG2E)).astype(_BF16)
    v_ref[...] = _dot(hn, wv_ref[...]).astype(_BF16)
    kt = lax.dot_general(wkt_ref[...], hn, (((1,), (1,)), ((), ())), preferred_element_type=_F32)
    kt_ref[...] = kt.astype(_BF16)


def _qkv(x, meta, tp, g, wq, wkt, wv):
    b, seq, d = x.shape
    nt = tp // TIME_BLOCK
    pad_rows = tp - N_META - seq
    assert nt >= 2 and seq >= TIME_BLOCK and all(n % SUBLANES == 0 for n in (pad_rows, N_META, seq, TIME_BLOCK))
    row_spec = pl.BlockSpec((None, TIME_BLOCK, d), lambda i, j: (i, j, 0))
    tiles, meta_tiles = TIME_BLOCK // SUBLANES, N_META // SUBLANES
    x_spec = pl.BlockSpec(
        (pl.Squeezed(), pl.Element(TIME_BLOCK), pl.Element(d)),
        lambda i, j: (i, jnp.clip(j * tiles - meta_tiles, 0, (seq - TIME_BLOCK) // SUBLANES) * SUBLANES, 0))
    return pl.pallas_call(
        functools.partial(_qkv_kernel, pad_rows=pad_rows),
        grid=(b, nt),
        in_specs=[x_spec, _resident((N_META, d)), _resident((1, d)), _resident((d, d)), _resident((d, d)),
                  _resident((d, d))],
        out_specs=[row_spec, pl.BlockSpec((None, d, TIME_BLOCK), lambda i, j: (i, 0, j)), row_spec, row_spec],
        out_shape=[jax.ShapeDtypeStruct((b, tp, d), _BF16), jax.ShapeDtypeStruct((b, d, tp), _BF16),
                   jax.ShapeDtypeStruct((b, tp, d), _BF16), jax.ShapeDtypeStruct((b, tp, d), _F32)],
        compiler_params=pltpu.CompilerParams(dimension_semantics=("parallel", "parallel"),
                                             vmem_limit_bytes=VMEM_LIMIT),
        name="qkv",
    )(x, meta, g, wq, wkt, wv)


def _attn_kernel(q_ref, kt_ref, v_ref, nu_ref, o_ref, acc_ref, carry_ref, z_ref, w_ref):
    blk, qrows, group = ATTN_BLOCK, ATTN_Q_ROWS, ATTN_KEY_GROUP
    lanes = q_ref.shape[1]
    n_heads = lanes // HEAD_DIM

    def head_of_lane(rows):
        return lax.broadcasted_iota(jnp.int32, (rows, lanes), 1) // HEAD_DIM

    def stacked_values(key_blk, keys):
        v = v_ref[pl.ds(pl.multiple_of(key_blk * blk, blk), keys), :]
        v_head = head_of_lane(keys)
        return jnp.concatenate([jnp.where(v_head == hd, v, jnp.zeros_like(v)) for hd in range(n_heads)], axis=0)

    def scores(q_head, key_blk, keys):
        return _dot(q_head, kt_ref[:, pl.ds(pl.multiple_of(key_blk * blk, blk), keys)])

    def visibility(rows, n_blk, causal_shift):
        masks = []
        for k_idx in range(n_blk):
            if causal_shift is None or (k_idx + 1) * blk - causal_shift <= 0:
                masks.append(None)
            else:
                key_pos = lax.broadcasted_iota(jnp.int32, (rows, blk), 1) + (k_idx * blk - causal_shift)
                masks.append(key_pos < lax.broadcasted_iota(jnp.int32, (rows, blk), 0))
        return masks

    def softplus_keys(z, masks):
        sp = jnp.maximum(jnp.log2(1.0 + jnp.exp2(jnp.minimum(z, 100.0))), z)
        return jnp.concatenate([sp[:, k * blk:(k + 1) * blk] if m is None else
                                jnp.where(m, sp[:, k * blk:(k + 1) * blk], 0.0)
                                for k, m in enumerate(masks)], axis=1).astype(_BF16)

    def suffix_sums(sp, widths):
        out, lo = [], 0
        for width in widths:
            out.append(_dot(sp[:, lo:lo + width], nu_ref[:width, :width]))
            lo += width
        return out

    def weights_from(z, suffixes, hd, rows, widths, masks):
        carry = carry_ref[hd, :rows, :]
        w_head = [None] * len(masks)
        lo = sum(widths)
        for width, suffix in zip(reversed(widths), reversed(suffixes)):
            lo -= width
            for c in range(width // blk):
                k_idx = lo // blk + c
                w = jnp.exp2(z[:, k_idx * blk:(k_idx + 1) * blk] + suffix[:, c * blk:(c + 1) * blk] + carry)
                if masks[k_idx] is not None:
                    w = jnp.where(masks[k_idx], w, 0.0)
                w_head[k_idx] = w.astype(_BF16)
            carry = carry + jnp.broadcast_to(suffix[:, 0:1], (rows, LANES))
        carry_ref[hd, :rows, :] = carry
        return w_head

    def weights(z, hd, rows, widths, masks):
        return weights_from(z, suffix_sums(softplus_keys(z, masks), widths), hd, rows, widths, masks)

    def sweep(qh, rows, key_blk, widths, causal_shift, between=None):
        keys = sum(widths)
        masks = visibility(rows, keys // blk, causal_shift)
        zs = [scores(qh[hd], key_blk, keys) for hd in range(n_heads)]
        if between is not None:
            between()
        suffixes = [suffix_sums(softplus_keys(z, masks), widths) for z in zs]
        ws = []
        for hd in range(n_heads):
            ws += weights_from(zs[hd], suffixes[hd], hd, rows, widths, masks)
        acc_ref[:rows, :] += _dot(jnp.concatenate(ws, axis=1), stacked_values(key_blk, keys))

    def start_block(row0, rows):
        q = q_ref[pl.ds(row0, rows), :]
        q_head = head_of_lane(rows)
        acc_ref[...] = jnp.zeros_like(acc_ref)
        carry_ref[...] = jnp.zeros_like(carry_ref)
        return [jnp.where(q_head == hd, q, jnp.zeros_like(q)) for hd in range(n_heads)]

    def write(row0, rows):
        o_ref[pl.ds(row0, rows), :] = acc_ref[:rows, :].astype(o_ref.dtype)

    sweep(start_block(0, blk), blk, 0, [blk], 0)
    write(0, blk)

    group_widths = [ATTN_SUFFIX_BLOCK] * (group * blk // ATTN_SUFFIX_BLOCK)
    group_keys = group * blk
    assert group_keys == qrows

    def wide_block(qi, _):
        row0 = pl.multiple_of(blk + qi * qrows, blk)
        first = 1 + qi * group
        qh = start_block(row0, qrows)

        def item_start(g):
            return jnp.maximum(first - (g + 1) * group, 0)

        def store_weights(w_head, hd):
            w_ref[:, hd * group_keys:(hd + 1) * group_keys] = jnp.concatenate(w_head, axis=1)

        def pv(g):
            acc_ref[...] += _dot(w_ref[...], stacked_values(item_start(g), group_keys))

        diag_masks = visibility(qrows, group, 0)
        for hd in range(n_heads):
            store_weights(weights(scores(qh[hd], first, group_keys), hd, qrows, group_widths, diag_masks), hd)
            z_ref[hd] = scores(qh[hd], item_start(0), group_keys)

        def key_group(state):
            g, _ = state
            pv(g - 1)
            no_mask = [None] * group
            suffixes = suffix_sums(softplus_keys(z_ref[0], no_mask), group_widths)
            for hd in range(n_heads):
                ahead = (suffix_sums(softplus_keys(z_ref[hd + 1], no_mask), group_widths)
                         if hd + 1 < n_heads else None)
                store_weights(weights_from(z_ref[hd], suffixes, hd, qrows, group_widths, no_mask), hd)
                z_ref[hd] = scores(qh[hd], item_start(g + 1), group_keys)
                suffixes = ahead
            live = (jnp.max(carry_ref[...]) > ATTN_LOG2_WEIGHT_FLOOR).astype(jnp.int32)
            return g + 1, live

        n_done, live = lax.while_loop(lambda state: jnp.logical_and(state[0] < qi, state[1] > 0), key_group,
                                      (jnp.int32(0), jnp.int32(1)))

        @pl.when(live > 0)
        def _():
            sweep(qh, qrows, 0, [blk], None, between=lambda: pv(n_done - 1))

        @pl.when(live == 0)
        def _():
            pv(n_done - 1)

        write(row0, qrows)
        return 0

    lax.fori_loop(0, (q_ref.shape[0] - blk) // qrows, wide_block, 0)


def _suffix_matrix():
    j = jnp.arange(ATTN_SUFFIX_BLOCK)[:, None]
    s = jnp.arange(ATTN_SUFFIX_BLOCK)[None, :]
    return -(j >= s).astype(_BF16)


def _sb_attention(q, kt, v):
    b, tp, d = q.shape
    assert (tp - ATTN_BLOCK) % ATTN_Q_ROWS == 0 and (ATTN_Q_ROWS // ATTN_BLOCK) % ATTN_KEY_GROUP == 0
    lanes = ATTN_HEADS_PER_STEP * HEAD_DIM
    return pl.pallas_call(
        _attn_kernel,
        grid=(b, d // lanes),
        in_specs=[pl.BlockSpec((None, tp, lanes), lambda i, j: (i, 0, j)),
                  pl.BlockSpec((None, lanes, tp), lambda i, j: (i, j, 0)),
                  pl.BlockSpec((None, tp, lanes), lambda i, j: (i, 0, j)),
                  _resident((ATTN_SUFFIX_BLOCK, ATTN_SUFFIX_BLOCK))],
        out_specs=pl.BlockSpec((None, tp, lanes), lambda i, j: (i, 0, j)),
        out_shape=jax.ShapeDtypeStruct((b, tp, d), _BF16),
        scratch_shapes=[pltpu.VMEM((ATTN_Q_ROWS, lanes), _F32),
                        pltpu.VMEM((ATTN_HEADS_PER_STEP, ATTN_Q_ROWS, LANES), _F32),
                        pltpu.VMEM((ATTN_HEADS_PER_STEP, ATTN_Q_ROWS, ATTN_KEY_GROUP * ATTN_BLOCK), _F32),
                        pltpu.VMEM((ATTN_Q_ROWS, ATTN_HEADS_PER_STEP * ATTN_KEY_GROUP * ATTN_BLOCK), _BF16)],
        compiler_params=pltpu.CompilerParams(dimension_semantics=("parallel", "parallel"),
                                             vmem_limit_bytes=VMEM_LIMIT),
        name="sb_attn",
    )(q, kt, v, _suffix_matrix())


def _gelu_tanh(x):
    return 0.5 * x * (1.0 + jnp.tanh(math.sqrt(2.0 / math.pi) * (x + 0.044715 * (x * x * x))))


def _sigmoid(x):
    return 1.0 / (1.0 + jnp.exp(-x))


def _lru_stage(h_ref, g_ref, win_ref, cw_ref, cb_ref, wg_ref, brg_ref, big_ref, lam_ref, y_ref,
               rec_ref, a_ref, b_ref, state_ref):
    rows, d = h_ref.shape
    tail = SUBLANES

    hn = _rms_norm(h_ref[...], g_ref[...]).astype(_BF16)
    gate = _gelu_tanh(_dot(hn, win_ref[:, :d]))
    rec_ref[tail:, :] = _dot(hn, win_ref[:, d:])
    yield

    u = cb_ref[...] + cw_ref[CONV_WIDTH - 1:CONV_WIDTH, :] * rec_ref[tail:, :]
    for j in range(CONV_WIDTH - 1):
        shift = CONV_WIDTH - 1 - j
        u = u + cw_ref[j:j + 1, :] * rec_ref[tail - shift:tail - shift + rows, :]
    rec_ref[0:tail, :] = rec_ref[rows:rows + tail, :]
    yield

    neg_lam = -lam_ref[...]
    log_a_unit = -LRU_C * (jnp.maximum(neg_lam, 0.0) + jnp.log1p(jnp.exp(-jnp.abs(neg_lam))))
    ub = u.astype(_BF16)
    for n in range(LRU_BLOCKS):
        cols = slice(n * LRU_BLOCK_DIM, (n + 1) * LRU_BLOCK_DIM)
        ri = _dot(ub[:, cols], wg_ref[n])
        r = _sigmoid(ri[:, :LRU_BLOCK_DIM] + brg_ref[:, cols])
        i = _sigmoid(ri[:, LRU_BLOCK_DIM:] + big_ref[:, cols])
        log_a = log_a_unit[:, cols] * r
        a = jnp.exp(log_a)
        one_minus_a2 = -jnp.tanh(log_a) * (1.0 + a * a)
        mult = jnp.where(one_minus_a2 > 0.0, one_minus_a2 * lax.rsqrt(one_minus_a2), 0.0)
        a_ref[:, cols] = a
        b_ref[:, cols] = mult * (i * u[:, cols])
        yield

    sub = lax.broadcasted_iota(jnp.int32, (SUBLANES, d), 0)
    state = state_ref[...]
    for gidx in range(rows // SUBLANES):
        r0 = gidx * SUBLANES
        av = a_ref[r0:r0 + SUBLANES, :]
        bv = b_ref[r0:r0 + SUBLANES, :]
        for s in (1, 2, 4):
            keep = sub >= s
            bv = jnp.where(keep, av * pltpu.roll(bv, s, axis=0) + bv, bv)
            av = jnp.where(keep, av * pltpu.roll(av, s, axis=0), av)
        hv = av * state + bv
        b_ref[r0:r0 + SUBLANES, :] = hv
        state = jnp.broadcast_to(hv[SUBLANES - 1:SUBLANES, :], (SUBLANES, d))
        if gidx % LRU_SCAN_GROUPS_PER_PIECE == LRU_SCAN_GROUPS_PER_PIECE - 1:
            yield
    state_ref[...] = state
    y_ref[...] = (b_ref[...] * gate).astype(y_ref.dtype)


def _mlp_stage(y, h, wo_ref, g_ref, wup_ref, wdn_ref, gf_ref, o_ref, hn_ref, final_norm, ff_chunk):
    d_ff = wup_ref.shape[1]
    h1 = h + _dot(y, wo_ref[...])
    hn_ref[...] = _rms_norm(h1, g_ref[...]).astype(_BF16)
    o_ref[...] = h1
    yield
    for c in range(0, d_ff, ff_chunk):
        up = jnp.maximum(_dot(hn_ref[...], wup_ref[:, c:c + ff_chunk]), 0.0)
        up = (up * up).astype(_BF16)
        yield
        o_ref[...] += _dot(up, wdn_ref[c:c + ff_chunk, :])
        yield
    if final_norm:
        o_ref[...] = _rms_norm(o_ref[...], gf_ref[...])


def _interleave(*stages):
    live = list(stages)
    while live:
        for stage in list(live):
            if next(stage, StopIteration) is StopIteration:
                live.remove(stage)


def _mlp_kernel(y_ref, h_ref, wo_ref, g_ref, wup_ref, wdn_ref, gf_ref, o_ref, hn_ref, *, final_norm):
    _interleave(_mlp_stage(y_ref[...], h_ref[...], wo_ref, g_ref, wup_ref, wdn_ref, gf_ref, o_ref, hn_ref,
                           final_norm, MLP_FF_CHUNK))


def _proj_mlp(y, h, w_o, g, w_up, w_down, g_final, final_norm):
    m, d = h.shape
    d_ff = w_up.shape[1]
    row_spec = pl.BlockSpec((MLP_ROWS, d), lambda i: (i, 0))
    return pl.pallas_call(
        functools.partial(_mlp_kernel, final_norm=final_norm),
        grid=(m // MLP_ROWS,),
        in_specs=[row_spec, row_spec, _resident((d, d)), _resident((1, d)), _resident((d, d_ff)),
                  _resident((d_ff, d)), _resident((1, d))],
        out_specs=row_spec,
        out_shape=jax.ShapeDtypeStruct((m, d), _F32),
        scratch_shapes=[pltpu.VMEM((MLP_ROWS, d), _BF16)],
        compiler_params=pltpu.CompilerParams(dimension_semantics=("parallel",), vmem_limit_bytes=VMEM_LIMIT),
        name="proj_mlp",
    )(y, h, w_o, g, w_up, w_down, g_final)


def _lru_layer_kernel(hcur_ref, hprev_ref, g_ref, win_ref, cw_ref, cb_ref, wg_ref, brg_ref, big_ref, lam_ref,
                      wo_ref, gm_ref, wup_ref, wdn_ref, gf_ref, o_ref,
                      rec_ref, a_ref, b_ref, state_ref, y_ref, hn_ref, *, steps_per_batch):
    s = pl.program_id(0)

    @pl.when(s == 0)
    def _():
        y_ref[...] = jnp.zeros_like(y_ref)

    @pl.when(s % steps_per_batch == 0)
    def _():
        rec_ref[0:SUBLANES, :] = jnp.zeros((SUBLANES, rec_ref.shape[1]), _F32)
        state_ref[...] = jnp.zeros_like(state_ref)

    y_prev = y_ref[...]
    _interleave(
        _lru_stage(hcur_ref, g_ref, win_ref, cw_ref, cb_ref, wg_ref, brg_ref, big_ref, lam_ref, y_ref,
                   rec_ref, a_ref, b_ref, state_ref),
        _mlp_stage(y_prev, hprev_ref[...], wo_ref, gm_ref, wup_ref, wdn_ref, gf_ref, o_ref, hn_ref, True,
                   LRU_LAYER_FF_CHUNK))


def _lru_layer(h, steps_per_batch, g, w_in, conv_w, conv_b, w_gates, b_rg, b_ig, lam, w_out, g_mlp, w_up, w_down,
               g_final):
    m, d = h.shape
    d_ff = w_up.shape[1]
    n_blocks = m // TIME_BLOCK
    cur_spec = pl.BlockSpec((TIME_BLOCK, d), lambda s: (jnp.minimum(s, n_blocks - 1), 0))
    prev_spec = pl.BlockSpec((TIME_BLOCK, d), lambda s: (jnp.maximum(s - 1, 0), 0))
    return pl.pallas_call(
        functools.partial(_lru_layer_kernel, steps_per_batch=steps_per_batch),
        grid=(n_blocks + 1,),
        in_specs=[cur_spec, prev_spec, _resident((1, d)), _resident((d, 2 * d)), _resident((CONV_WIDTH, d)),
                  _resident((1, d)), _resident((LRU_BLOCKS, LRU_BLOCK_DIM, 2 * LRU_BLOCK_DIM)),
                  _resident((1, d)), _resident((1, d)), _resident((1, d)),
                  _resident((d, d)), _resident((1, d)), _resident((d, d_ff)), _resident((d_ff, d)),
                  _resident((1, d))],
        out_specs=prev_spec,
        out_shape=jax.ShapeDtypeStruct((m, d), _F32),
        scratch_shapes=[pltpu.VMEM((SUBLANES + TIME_BLOCK, d), _F32), pltpu.VMEM((TIME_BLOCK, d), _F32),
                        pltpu.VMEM((TIME_BLOCK, d), _F32), pltpu.VMEM((SUBLANES, d), _F32),
                        pltpu.VMEM((TIME_BLOCK, d), _BF16), pltpu.VMEM((TIME_BLOCK, d), _BF16)],
        compiler_params=pltpu.CompilerParams(dimension_semantics=("arbitrary",), vmem_limit_bytes=VMEM_LIMIT),
        name="lru_layer",
    )(h, h, g, w_in, conv_w, conv_b, w_gates, b_rg, b_ig, lam, w_out, g_mlp, w_up, w_down, g_final)


def kernel(x, meta_tokens, norm_mix, norm_mlp, sb_w_qkv, sb_w_o, lru_w_in, lru_conv_w, lru_conv_b, lru_w_rg,
           lru_b_rg, lru_w_ig, lru_b_ig, lru_lambda, lru_w_out, mlp_w_up, mlp_w_down, norm_final):
    b, seq, d = x.shape
    assert d == HEADS * HEAD_DIM == LRU_BLOCKS * LRU_BLOCK_DIM
    t_len = N_META + seq
    tp = -(-t_len // TIME_BLOCK) * TIME_BLOCK
    assert (b * tp) % MLP_ROWS == 0 and tp % ATTN_BLOCK == 0

    row = lambda v: v.reshape(1, d)
    w_qkv = sb_w_qkv[0]
    q, kt, v, h = _qkv(x, meta_tokens.astype(x.dtype), tp, row(norm_mix[0]), w_qkv[:, :d].astype(_BF16),
                       w_qkv[:, d:2 * d].T.astype(_BF16), w_qkv[:, 2 * d:].astype(_BF16))
    o = _sb_attention(q, kt, v)
    h = _proj_mlp(o.reshape(b * tp, d), h.reshape(b * tp, d), sb_w_o[0].astype(_BF16), row(norm_mlp[0]),
                  mlp_w_up[0].astype(_BF16), mlp_w_down[0].astype(_BF16), row(norm_final), False)

    w_gates = jnp.concatenate([lru_w_rg[0], lru_w_ig[0]], axis=-1).astype(_BF16)
    h = _lru_layer(h, tp // TIME_BLOCK, row(norm_mix[1]), lru_w_in[0].astype(_BF16), lru_conv_w[0],
                   row(lru_conv_b[0]), w_gates, row(lru_b_rg[0]), row(lru_b_ig[0]), row(lru_lambda[0]),
                   lru_w_out[0].astype(_BF16), row(norm_mlp[1]), mlp_w_up[1].astype(_BF16),
                   mlp_w_down[1].astype(_BF16), row(norm_final))
    return h.reshape(b, tp, d)[:, N_META:t_len]
```

```python
import functools
import math

import jax
import jax.numpy as jnp
from jax import lax
from jax.experimental import pallas as pl
from jax.experimental.pallas import tpu as pltpu

N_META = 16
HEADS = 16
HEAD_DIM = 64
LRU_BLOCKS = 8
LRU_BLOCK_DIM = 128
CONV_WIDTH = 4
LRU_C = 8.0
EPS = 1e-6
LOG2E = 1.4426950408889634

LANES = 128
SUBLANES = 8
TIME_BLOCK = 384
ATTN_BLOCK = 128
ATTN_Q_ROWS = 512
ATTN_KEY_GROUP = 4
ATTN_SUFFIX_BLOCK = 256
ATTN_HEADS_PER_STEP = 4
ATTN_LOG2_WEIGHT_FLOOR = -256.0
MLP_ROWS = 512
MLP_FF_CHUNK = 1024
LRU_SCAN_GROUPS_PER_PIECE = 8
LRU_LAYER_FF_CHUNK = 512
VMEM_LIMIT = 56 * 1024 * 1024

_BF16 = jnp.bfloat16
_F32 = jnp.float32


def _dot(a, b):
    return jnp.dot(a, b, preferred_element_type=_F32)


def _rms_norm(x, g):
    ms = jnp.mean(x * x, axis=-1, keepdims=True)
    return (x * lax.rsqrt(ms + EPS)) * g


def _resident(shape):
    zeros = (0,) * len(shape)
    return pl.BlockSpec(shape, lambda *_: zeros, pipeline_mode=pl.Buffered(1))


def _qkv_kernel(x_ref, meta_ref, g_ref, wq_ref, wkt_ref, wv_ref, q_ref, kt_ref, v_ref, h_ref, *, pad_rows):
    j, last = pl.program_id(1), pl.num_programs(1) - 1
    rows, d = x_ref.shape
    x = x_ref[...]
    first_block = jnp.concatenate([meta_ref[...], x[:rows - N_META]], axis=0)
    last_block = jnp.concatenate([x[pad_rows:], jnp.zeros((pad_rows, d), x.dtype)], axis=0)
    h = jnp.where(j == 0, first_block, jnp.where(j == last, last_block, x))
    h_ref[...] = h
    hn = _rms_norm(h, g_ref[...]).astype(_BF16)
    q_ref[...] = (_dot(hn, wq_ref[...]) * (HEAD_DIM ** -0.5 * LOG2E)).astype(_BF16)
    v_ref[...] = _dot(hn, wv_ref[...]).astype(_BF16)
    kt = lax.dot_general(wkt_ref[...], hn, (((1,), (1,)), ((), ())), preferred_element_type=_F32)
    kt_ref[...] = kt.astype(_BF16)


def _qkv(x, meta, tp, g, wq, wkt, wv):
    b, seq, d = x.shape
    nt = tp // TIME_BLOCK
    pad_rows = tp - N_META - seq
    assert nt >= 2 and seq >= TIME_BLOCK and all(n % SUBLANES == 0 for n in (pad_rows, N_META, seq, TIME_BLOCK))
    row_spec = pl.BlockSpec((None, TIME_BLOCK, d), lambda i, j: (i, j, 0))
    tiles, meta_tiles = TIME_BLOCK // SUBLANES, N_META // SUBLANES
    x_spec = pl.BlockSpec(
        (pl.Squeezed(), pl.Element(TIME_BLOCK), pl.Element(d)),
        lambda i, j: (i, jnp.clip(j * tiles - meta_tiles, 0, (seq - TIME_BLOCK) // SUBLANES) * SUBLANES, 0))
    return pl.pallas_call(
        functools.partial(_qkv_kernel, pad_rows=pad_rows),
        grid=(b, nt),
        in_specs=[x_spec, _resident((N_META, d)), _resident((1, d)), _resident((d, d)), _resident((d, d)),
                  _resident((d, d))],
        out_specs=[row_spec, pl.BlockSpec((None, d, TIME_BLOCK), lambda i, j: (i, 0, j)), row_spec, row_spec],
        out_shape=[jax.ShapeDtypeStruct((b, tp, d), _BF16), jax.ShapeDtypeStruct((b, d, tp), _BF16),
                   jax.ShapeDtypeStruct((b, tp, d), _BF16), jax.ShapeDtypeStruct((b, tp, d), _F32)],
        compiler_params=pltpu.CompilerParams(dimension_semantics=("parallel", "parallel"),
                                             vmem_limit_bytes=VMEM_LIMIT),
        name="qkv",
    )(x, meta, g, wq, wkt, wv)


def _attn_kernel(q_ref, kt_ref, v_ref, nu_ref, o_ref, acc_ref, carry_ref, z_ref, w_ref):
    blk, qrows, group = ATTN_BLOCK, ATTN_Q_ROWS, ATTN_KEY_GROUP
    lanes = q_ref.shape[1]
    n_heads = lanes // HEAD_DIM

    def head_of_lane(rows):
        return lax.broadcasted_iota(jnp.int32, (rows, lanes), 1) // HEAD_DIM

    def stacked_values(key_blk, keys):
        v = v_ref[pl.ds(pl.multiple_of(key_blk * blk, blk), keys), :]
        v_head = head_of_lane(keys)
        return jnp.concatenate([jnp.where(v_head == hd, v, jnp.zeros_like(v)) for hd in range(n_heads)], axis=0)

    def scores(q_head, key_blk, keys):
        return _dot(q_head, kt_ref[:, pl.ds(pl.multiple_of(key_blk * blk, blk), keys)])

    def visibility(rows, n_blk, causal_shift):
        masks = []
        for k_idx in range(n_blk):
            if causal_shift is None or (k_idx + 1) * blk - causal_shift <= 0:
                masks.append(None)
            else:
                key_pos = lax.broadcasted_iota(jnp.int32, (rows, blk), 1) + (k_idx * blk - causal_shift)
                masks.append(key_pos < lax.broadcasted_iota(jnp.int32, (rows, blk), 0))
        return masks

    def softplus_keys(z, masks):
        sp = jnp.maximum(jnp.log2(1.0 + jnp.exp2(jnp.minimum(z, 100.0))), z)
        return jnp.concatenate([sp[:, k * blk:(k + 1) * blk] if m is None else
                                jnp.where(m, sp[:, k * blk:(k + 1) * blk], 0.0)
                                for k, m in enumerate(masks)], axis=1).astype(_BF16)

    def suffix_sums(sp, widths):
        out, lo = [], 0
        for width in widths:
            out.append(_dot(sp[:, lo:lo + width], nu_ref[:width, :width]))
            lo += width
        return out

    def weights_from(z, suffixes, hd, rows, widths, masks):
        carry = carry_ref[hd, :rows, :]
        w_head = [None] * len(masks)
        lo = sum(widths)
        for width, suffix in zip(reversed(widths), reversed(suffixes)):
            lo -= width
            for c in range(width // blk):
                k_idx = lo // blk + c
                w = jnp.exp2(z[:, k_idx * blk:(k_idx + 1) * blk] + suffix[:, c * blk:(c + 1) * blk] + carry)
                if masks[k_idx] is not None:
                    w = jnp.where(masks[k_idx], w, 0.0)
                w_head[k_idx] = w.astype(_BF16)
            carry = carry + jnp.broadcast_to(suffix[:, 0:1], (rows, LANES))
        carry_ref[hd, :rows, :] = carry
        return w_head

    def sweep(qh, rows, key_blk, widths, causal_shift):
        keys = sum(widths)
        masks = visibility(rows, keys // blk, causal_shift)
        v = v_ref[pl.ds(pl.multiple_of(key_blk * blk, blk), keys), :]
        v_head = head_of_lane(keys)
        z = scores(qh[0], key_blk, keys)
        suffixes = suffix_sums(softplus_keys(z, masks), widths)
        for hd in range(n_heads):
            if hd + 1 < n_heads:
                z_ahead = scores(qh[hd + 1], key_blk, keys)
                suffixes_ahead = suffix_sums(softplus_keys(z_ahead, masks), widths)
            w = jnp.concatenate(weights_from(z, suffixes, hd, rows, widths, masks), axis=1)
            acc_ref[:rows, :] += _dot(w, jnp.where(v_head == hd, v, jnp.zeros_like(v)))
            if hd + 1 < n_heads:
                z, suffixes = z_ahead, suffixes_ahead

    def start_block(row0, rows):
        q = q_ref[pl.ds(row0, rows), :]
        q_head = head_of_lane(rows)
        acc_ref[...] = jnp.zeros_like(acc_ref)
        carry_ref[...] = jnp.zeros_like(carry_ref)
        return [jnp.where(q_head == hd, q, jnp.zeros_like(q)) for hd in range(n_heads)]

    def write(row0, rows):
        o_ref[pl.ds(row0, rows), :] = acc_ref[:rows, :].astype(o_ref.dtype)

    def weights_live():
        return (jnp.max(carry_ref[...]) > ATTN_LOG2_WEIGHT_FLOOR).astype(jnp.int32)

    wide = ATTN_SUFFIX_BLOCK
    group_widths = [wide] * (group * blk // wide)
    group_keys = group * blk
    near = wide // blk
    assert group_keys == qrows and 0 < near < group

    sweep(start_block(0, blk), blk, 0, [blk], 0)
    write(0, blk)
    sweep(start_block(blk, qrows), qrows, 0, [blk] + group_widths, blk)
    write(blk, qrows)

    def wide_block(qi, _):
        row0 = pl.multiple_of(blk + qi * qrows, blk)
        first = 1 + qi * group
        qh = start_block(row0, qrows)
        sweep(qh, qrows, first - near, [wide] + group_widths, wide)

        @pl.when(weights_live() > 0)
        def _():
            def item_start(g):
                return jnp.maximum(first - near - (g + 1) * group, 0)

            def pv(g):
                acc_ref[...] += _dot(w_ref[...], stacked_values(item_start(g), group_keys))

            for hd in range(n_heads):
                z_ref[hd] = scores(qh[hd], item_start(0), group_keys)
            w_ref[...] = jnp.zeros_like(w_ref)

            def key_group(state):
                g, _ = state
                pv(g - 1)
                no_mask = [None] * group
                suffixes = suffix_sums(softplus_keys(z_ref[0], no_mask), group_widths)
                for hd in range(n_heads):
                    ahead = (suffix_sums(softplus_keys(z_ref[hd + 1], no_mask), group_widths)
                             if hd + 1 < n_heads else None)
                    w_head = weights_from(z_ref[hd], suffixes, hd, qrows, group_widths, no_mask)
                    w_ref[:, hd * group_keys:(hd + 1) * group_keys] = jnp.concatenate(w_head, axis=1)
                    z_ref[hd] = scores(qh[hd], item_start(g + 1), group_keys)
                    suffixes = ahead
                return g + 1, weights_live()

            n_done, live = lax.while_loop(lambda state: jnp.logical_and(state[0] < qi - 1, state[1] > 0),
                                          key_group, (jnp.int32(0), jnp.int32(1)))
            pv(n_done - 1)

            @pl.when(live > 0)
            def _():
                sweep(qh, qrows, 0, [blk, wide], None)

        write(row0, qrows)
        return 0

    lax.fori_loop(1, (q_ref.shape[0] - blk) // qrows, wide_block, 0)


def _suffix_matrix():
    j = jnp.arange(ATTN_SUFFIX_BLOCK)[:, None]
    s = jnp.arange(ATTN_SUFFIX_BLOCK)[None, :]
    return -(j >= s).astype(_BF16)


def _sb_attention(q, kt, v):
    b, tp, d = q.shape
    assert (tp - ATTN_BLOCK) % ATTN_Q_ROWS == 0 and (ATTN_Q_ROWS // ATTN_BLOCK) % ATTN_KEY_GROUP == 0
    lanes = ATTN_HEADS_PER_STEP * HEAD_DIM
    return pl.pallas_call(
        _attn_kernel,
        grid=(b, d // lanes),
        in_specs=[pl.BlockSpec((None, tp, lanes), lambda i, j: (i, 0, j)),
                  pl.BlockSpec((None, lanes, tp), lambda i, j: (i, j, 0)),
                  pl.BlockSpec((None, tp, lanes), lambda i, j: (i, 0, j)),
                  _resident((ATTN_SUFFIX_BLOCK, ATTN_SUFFIX_BLOCK))],
        out_specs=pl.BlockSpec((None, tp, lanes), lambda i, j: (i, 0, j)),
        out_shape=jax.ShapeDtypeStruct((b, tp, d), _BF16),
        scratch_shapes=[pltpu.VMEM((ATTN_Q_ROWS, lanes), _F32),
                        pltpu.VMEM((ATTN_HEADS_PER_STEP, ATTN_Q_ROWS, LANES), _F32),
                        pltpu.VMEM((ATTN_HEADS_PER_STEP, ATTN_Q_ROWS, ATTN_KEY_GROUP * ATTN_BLOCK), _F32),
                        pltpu.VMEM((ATTN_Q_ROWS, ATTN_HEADS_PER_STEP * ATTN_KEY_GROUP * ATTN_BLOCK), _BF16)],
        compiler_params=pltpu.CompilerParams(dimension_semantics=("parallel", "parallel"),
                                             vmem_limit_bytes=VMEM_LIMIT),
        name="sb_attn",
    )(q, kt, v, _suffix_matrix())


def _gelu_tanh(x):
    return 0.5 * x * (1.0 + jnp.tanh(math.sqrt(2.0 / math.pi) * (x + 0.044715 * (x * x * x))))


def _sigmoid(x):
    return 1.0 / (1.0 + jnp.exp(-x))


def _lru_stage(h_ref, g_ref, win_ref, cw_ref, cb_ref, wg_ref, brg_ref, big_ref, lam_ref, y_ref,
               rec_ref, a_ref, b_ref, state_ref):
    rows, d = h_ref.shape
    tail = SUBLANES

    hn = _rms_norm(h_ref[...], g_ref[...]).astype(_BF16)
    gate = _gelu_tanh(_dot(hn, win_ref[:, :d]))
    rec_ref[tail:, :] = _dot(hn, win_ref[:, d:])
    yield

    u = cb_ref[...] + cw_ref[CONV_WIDTH - 1:CONV_WIDTH, :] * rec_ref[tail:, :]
    for j in range(CONV_WIDTH - 1):
        shift = CONV_WIDTH - 1 - j
        u = u + cw_ref[j:j + 1, :] * rec_ref[tail - shift:tail - shift + rows, :]
    rec_ref[0:tail, :] = rec_ref[rows:rows + tail, :]
    yield

    neg_lam = -lam_ref[...]
    log_a_unit = -LRU_C * (jnp.maximum(neg_lam, 0.0) + jnp.log1p(jnp.exp(-jnp.abs(neg_lam))))
    ub = u.astype(_BF16)
    for n in range(LRU_BLOCKS):
        cols = slice(n * LRU_BLOCK_DIM, (n + 1) * LRU_BLOCK_DIM)
        ri = _dot(ub[:, cols], wg_ref[n])
        r = _sigmoid(ri[:, :LRU_BLOCK_DIM] + brg_ref[:, cols])
        i = _sigmoid(ri[:, LRU_BLOCK_DIM:] + big_ref[:, cols])
        log_a = log_a_unit[:, cols] * r
        a = jnp.exp(log_a)
        one_minus_a2 = -jnp.tanh(log_a) * (1.0 + a * a)
        mult = jnp.where(one_minus_a2 > 0.0, one_minus_a2 * lax.rsqrt(one_minus_a2), 0.0)
        a_ref[:, cols] = a
        b_ref[:, cols] = mult * (i * u[:, cols])
        yield

    sub = lax.broadcasted_iota(jnp.int32, (SUBLANES, d), 0)
    state = state_ref[...]
    for gidx in range(rows // SUBLANES):
        r0 = gidx * SUBLANES
        av = a_ref[r0:r0 + SUBLANES, :]
        bv = b_ref[r0:r0 + SUBLANES, :]
        for s in (1, 2, 4):
            keep = sub >= s
            bv = jnp.where(keep, av * pltpu.roll(bv, s, axis=0) + bv, bv)
            av = jnp.where(keep, av * pltpu.roll(av, s, axis=0), av)
        hv = av * state + bv
        b_ref[r0:r0 + SUBLANES, :] = hv
        state = jnp.broadcast_to(hv[SUBLANES - 1:SUBLANES, :], (SUBLANES, d))
        if gidx % LRU_SCAN_GROUPS_PER_PIECE == LRU_SCAN_GROUPS_PER_PIECE - 1:
            yield
    state_ref[...] = state
    y_ref[...] = (b_ref[...] * gate).astype(y_ref.dtype)


def _mlp_stage(y, h, wo_ref, g_ref, wup_ref, wdn_ref, gf_ref, o_ref, hn_ref, final_norm, ff_chunk):
    d_ff = wup_ref.shape[1]
    h1 = h + _dot(y, wo_ref[...])
    hn_ref[...] = _rms_norm(h1, g_ref[...]).astype(_BF16)
    o_ref[...] = h1
    yield
    for c in range(0, d_ff, ff_chunk):
        up = jnp.maximum(_dot(hn_ref[...], wup_ref[:, c:c + ff_chunk]), 0.0)
        up = (up * up).astype(_BF16)
        yield
        o_ref[...] += _dot(up, wdn_ref[c:c + ff_chunk, :])
        yield
    if final_norm:
        o_ref[...] = _rms_norm(o_ref[...], gf_ref[...])


def _interleave(*stages):
    live = list(stages)
    while live:
        for stage in list(live):
            if next(stage, StopIteration) is StopIteration:
                live.remove(stage)


def _mlp_kernel(y_ref, h_ref, wo_ref, g_ref, wup_ref, wdn_ref, gf_ref, o_ref, hn_ref, *, final_norm):
    _interleave(_mlp_stage(y_ref[...], h_ref[...], wo_ref, g_ref, wup_ref, wdn_ref, gf_ref, o_ref, hn_ref,
                           final_norm, MLP_FF_CHUNK))


def _proj_mlp(y, h, w_o, g, w_up, w_down, g_final, final_norm):
    m, d = h.shape
    d_ff = w_up.shape[1]
    row_spec = pl.BlockSpec((MLP_ROWS, d), lambda i: (i, 0))
    return pl.pallas_call(
        functools.partial(_mlp_kernel, final_norm=final_norm),
        grid=(m // MLP_ROWS,),
        in_specs=[row_spec, row_spec, _resident((d, d)), _resident((1, d)), _resident((d, d_ff)),
                  _resident((d_ff, d)), _resident((1, d))],
        out_specs=row_spec,
        out_shape=jax.ShapeDtypeStruct((m, d), _F32),
        scratch_shapes=[pltpu.VMEM((MLP_ROWS, d), _BF16)],
        compiler_params=pltpu.CompilerParams(dimension_semantics=("parallel",), vmem_limit_bytes=VMEM_LIMIT),
        name="proj_mlp",
    )(y, h, w_o, g, w_up, w_down, g_final)


def _lru_layer_kernel(hcur_ref, hprev_ref, g_ref, win_ref, cw_ref, cb_ref, wg_ref, brg_ref, big_ref, lam_ref,
                      wo_ref, gm_ref, wup_ref, wdn_ref, gf_ref, o_ref,
                      rec_ref, a_ref, b_ref, state_ref, y_ref, hn_ref, *, steps_per_batch):
    s = pl.program_id(0)

    @pl.when(s == 0)
    def _():
        y_ref[...] = jnp.zeros_like(y_ref)

    @pl.when(s % steps_per_batch == 0)
    def _():
        rec_ref[0:SUBLANES, :] = jnp.zeros((SUBLANES, rec_ref.shape[1]), _F32)
        state_ref[...] = jnp.zeros_like(state_ref)

    y_prev = y_ref[...]
    _interleave(
        _lru_stage(hcur_ref, g_ref, win_ref, cw_ref, cb_ref, wg_ref, brg_ref, big_ref, lam_ref, y_ref,
                   rec_ref, a_ref, b_ref, state_ref),
        _mlp_stage(y_prev, hprev_ref[...], wo_ref, gm_ref, wup_ref, wdn_ref, gf_ref, o_ref, hn_ref, True,
                   LRU_LAYER_FF_CHUNK))


def _lru_layer(h, steps_per_batch, g, w_in, conv_w, conv_b, w_gates, b_rg, b_ig, lam, w_out, g_mlp, w_up, w_down,
               g_final):
    m, d = h.shape
    d_ff = w_up.shape[1]
    n_blocks = m // TIME_BLOCK
    cur_spec = pl.BlockSpec((TIME_BLOCK, d), lambda s: (jnp.minimum(s, n_blocks - 1), 0))
    prev_spec = pl.BlockSpec((TIME_BLOCK, d), lambda s: (jnp.maximum(s - 1, 0), 0))
    return pl.pallas_call(
        functools.partial(_lru_layer_kernel, steps_per_batch=steps_per_batch),
        grid=(n_blocks + 1,),
        in_specs=[cur_spec, prev_spec, _resident((1, d)), _resident((d, 2 * d)), _resident((CONV_WIDTH, d)),
                  _resident((1, d)), _resident((LRU_BLOCKS, LRU_BLOCK_DIM, 2 * LRU_BLOCK_DIM)),
                  _resident((1, d)), _resident((1, d)), _resident((1, d)),
                  _resident((d, d)), _resident((1, d)), _resident((d, d_ff)), _resident((d_ff, d)),
                  _resident((1, d))],
        out_specs=prev_spec,
        out_shape=jax.ShapeDtypeStruct((m, d), _F32),
        scratch_shapes=[pltpu.VMEM((SUBLANES + TIME_BLOCK, d), _F32), pltpu.VMEM((TIME_BLOCK, d), _F32),
                        pltpu.VMEM((TIME_BLOCK, d), _F32), pltpu.VMEM((SUBLANES, d), _F32),
                        pltpu.VMEM((TIME_BLOCK, d), _BF16), pltpu.VMEM((TIME_BLOCK, d), _BF16)],
        compiler_params=pltpu.CompilerParams(dimension_semantics=("arbitrary",), vmem_limit_bytes=VMEM_LIMIT),
        name="lru_layer",
    )(h, h, g, w_in, conv_w, conv_b, w_gates, b_rg, b_ig, lam, w_out, g_mlp, w_up, w_down, g_final)


def kernel(x, meta_tokens, norm_mix, norm_mlp, sb_w_qkv, sb_w_o, lru_w_in, lru_conv_w, lru_conv_b, lru_w_rg,
           lru_b_rg, lru_w_ig, lru_b_ig, lru_lambda, lru_w_out, mlp_w_up, mlp_w_down, norm_final):
    b, seq, d = x.shape
    assert d == HEADS * HEAD_DIM == LRU_BLOCKS * LRU_BLOCK_DIM
    t_len = N_META + seq
    tp = -(-t_len // TIME_BLOCK) * TIME_BLOCK
    assert (b * tp) % MLP_ROWS == 0 and tp % ATTN_BLOCK == 0

    row = lambda v: v.reshape(1, d)
    w_qkv = sb_w_qkv[0]
    q, kt, v, h = _qkv(x, meta_tokens.astype(x.dtype), tp, row(norm_mix[0]), w_qkv[:, :d].astype(_BF16),
                       w_qkv[:, d:2 * d].T.astype(_BF16), w_qkv[:, 2 * d:].astype(_BF16))
    o = _sb_attention(q, kt, v)
    h = _proj_mlp(o.reshape(b * tp, d), h.reshape(b * tp, d), sb_w_o[0].astype(_BF16), row(norm_mlp[0]),
                  mlp_w_up[0].astype(_BF16), mlp_w_down[0].astype(_BF16), row(norm_final), False)

    w_gates = jnp.concatenate([lru_w_rg[0], lru_w_ig[0]], axis=-1).astype(_BF16)
    h = _lru_layer(h, tp // TIME_BLOCK, row(norm_mix[1]), lru_w_in[0].astype(_BF16), lru_conv_w[0],
                   row(lru_conv_b[0]), w_gates, row(lru_b_rg[0]), row(lru_b_ig[0]), row(lru_lambda[0]),
                   lru_w_out[0].astype(_BF16), row(norm_mlp[1]), mlp_w_up[1].astype(_BF16),
                   mlp_w_down[1].astype(_BF16), row(norm_final))
    return h.reshape(b, tp, d)[:, N_META:t_len]
```

```python
import functools
import math

import jax
import jax.numpy as jnp
from jax import lax
from jax.experimental import pallas as pl
from jax.experimental.pallas import tpu as pltpu

N_META = 16
HEADS = 16
HEAD_DIM = 64
LRU_BLOCKS = 8
LRU_BLOCK_DIM = 128
CONV_WIDTH = 4
LRU_C = 8.0
EPS = 1e-6
LOG2E = 1.4426950408889634

LANES = 128
SUBLANES = 8
TIME_BLOCK = 384
ATTN_BLOCK = 128
ATTN_Q_ROWS = 256
ATTN_KEY_GROUP = 2
ATTN_SUFFIX_BLOCK = 256
ATTN_HEADS_PER_STEP = 4
ATTN_LOG2_WEIGHT_FLOOR = -192.0
MLP_ROWS = 512
MLP_FF_CHUNK = 1024
LRU_SCAN_GROUPS_PER_PIECE = 8
LRU_LAYER_FF_CHUNK = 512
VMEM_LIMIT = 56 * 1024 * 1024

_BF16 = jnp.bfloat16
_F32 = jnp.float32


def _dot(a, b):
    return jnp.dot(a, b, preferred_element_type=_F32)


def _rms_norm(x, g):
    ms = jnp.mean(x * x, axis=-1, keepdims=True)
    return (x * lax.rsqrt(ms + EPS)) * g


def _resident(shape):
    zeros = (0,) * len(shape)
    return pl.BlockSpec(shape, lambda *_: zeros, pipeline_mode=pl.Buffered(1))


def _qkv_kernel(x_ref, meta_ref, g_ref, wq_ref, wkt_ref, wv_ref, q_ref, kt_ref, v_ref, h_ref, *, pad_rows):
    j, last = pl.program_id(1), pl.num_programs(1) - 1
    rows, d = x_ref.shape
    x = x_ref[...]
    first_block = jnp.concatenate([meta_ref[...], x[:rows - N_META]], axis=0)
    last_block = jnp.concatenate([x[pad_rows:], jnp.zeros((pad_rows, d), x.dtype)], axis=0)
    h = jnp.where(j == 0, first_block, jnp.where(j == last, last_block, x))
    h_ref[...] = h
    hn = _rms_norm(h, g_ref[...]).astype(_BF16)
    q_ref[...] = (_dot(hn, wq_ref[...]) * (HEAD_DIM ** -0.5 * LOG2E)).astype(_BF16)
    v_ref[...] = _dot(hn, wv_ref[...]).astype(_BF16)
    kt = lax.dot_general(wkt_ref[...], hn, (((1,), (1,)), ((), ())), preferred_element_type=_F32)
    kt_ref[...] = kt.astype(_BF16)


def _qkv(x, meta, tp, g, wq, wkt, wv):
    b, seq, d = x.shape
    nt = tp // TIME_BLOCK
    pad_rows = tp - N_META - seq
    assert nt >= 2 and seq >= TIME_BLOCK and all(n % SUBLANES == 0 for n in (pad_rows, N_META, seq, TIME_BLOCK))
    row_spec = pl.BlockSpec((None, TIME_BLOCK, d), lambda i, j: (i, j, 0))
    tiles, meta_tiles = TIME_BLOCK // SUBLANES, N_META // SUBLANES
    x_spec = pl.BlockSpec(
        (pl.Squeezed(), pl.Element(TIME_BLOCK), pl.Element(d)),
        lambda i, j: (i, jnp.clip(j * tiles - meta_tiles, 0, (seq - TIME_BLOCK) // SUBLANES) * SUBLANES, 0))
    return pl.pallas_call(
        functools.partial(_qkv_kernel, pad_rows=pad_rows),
        grid=(b, nt),
        in_specs=[x_spec, _resident((N_META, d)), _resident((1, d)), _resident((d, d)), _resident((d, d)),
                  _resident((d, d))],
        out_specs=[row_spec, pl.BlockSpec((None, d, TIME_BLOCK), lambda i, j: (i, 0, j)), row_spec, row_spec],
        out_shape=[jax.ShapeDtypeStruct((b, tp, d), _BF16), jax.ShapeDtypeStruct((b, d, tp), _BF16),
                   jax.ShapeDtypeStruct((b, tp, d), _BF16), jax.ShapeDtypeStruct((b, tp, d), _F32)],
        compiler_params=pltpu.CompilerParams(dimension_semantics=("parallel", "parallel"),
                                             vmem_limit_bytes=VMEM_LIMIT),
        name="qkv",
    )(x, meta, g, wq, wkt, wv)


def _attn_kernel(q_ref, kt_ref, v_ref, nu_ref, o_ref, acc_ref, carry_ref, z_ref, w_ref):
    blk, qrows, group = ATTN_BLOCK, ATTN_Q_ROWS, ATTN_KEY_GROUP
    lanes = q_ref.shape[1]
    n_heads = lanes // HEAD_DIM

    def head_of_lane(rows):
        return lax.broadcasted_iota(jnp.int32, (rows, lanes), 1) // HEAD_DIM

    def stacked_values(key_blk, keys):
        v = v_ref[pl.ds(pl.multiple_of(key_blk * blk, blk), keys), :]
        v_head = head_of_lane(keys)
        return jnp.concatenate([jnp.where(v_head == hd, v, jnp.zeros_like(v)) for hd in range(n_heads)], axis=0)

    def scores(q_head, key_blk, keys):
        return _dot(q_head, kt_ref[:, pl.ds(pl.multiple_of(key_blk * blk, blk), keys)])

    def visibility(rows, n_blk, causal_shift):
        masks = []
        for k_idx in range(n_blk):
            if causal_shift is None or (k_idx + 1) * blk - causal_shift <= 0:
                masks.append(None)
            else:
                key_pos = lax.broadcasted_iota(jnp.int32, (rows, blk), 1) + (k_idx * blk - causal_shift)
                masks.append(key_pos < lax.broadcasted_iota(jnp.int32, (rows, blk), 0))
        return masks

    def softplus_keys(z, masks):
        sp = jnp.maximum(jnp.log2(1.0 + jnp.exp2(jnp.minimum(z, 100.0))), z)
        return jnp.concatenate([sp[:, k * blk:(k + 1) * blk] if m is None else
                                jnp.where(m, sp[:, k * blk:(k + 1) * blk], 0.0)
                                for k, m in enumerate(masks)], axis=1).astype(_BF16)

    def suffix_sums(sp, widths):
        out, lo = [], 0
        for width in widths:
            out.append(_dot(sp[:, lo:lo + width], nu_ref[:width, :width]))
            lo += width
        return out

    def weights_from(z, suffixes, hd, rows, widths, masks):
        carry = carry_ref[hd, :rows, :]
        w_head = [None] * len(masks)
        lo = sum(widths)
        for width, suffix in zip(reversed(widths), reversed(suffixes)):
            lo -= width
            for c in range(width // blk):
                k_idx = lo // blk + c
                w = jnp.exp2(z[:, k_idx * blk:(k_idx + 1) * blk] + suffix[:, c * blk:(c + 1) * blk] + carry)
                if masks[k_idx] is not None:
                    w = jnp.where(masks[k_idx], w, 0.0)
                w_head[k_idx] = w.astype(_BF16)
            carry = carry + jnp.broadcast_to(suffix[:, 0:1], (rows, LANES))
        carry_ref[hd, :rows, :] = carry
        return w_head

    def sweep(qh, rows, key_blk, widths, causal_shift):
        keys = sum(widths)
        masks = visibility(rows, keys // blk, causal_shift)
        v = v_ref[pl.ds(pl.multiple_of(key_blk * blk, blk), keys), :]
        v_head = head_of_lane(keys)
        z = scores(qh[0], key_blk, keys)
        suffixes = suffix_sums(softplus_keys(z, masks), widths)
        for hd in range(n_heads):
            if hd + 1 < n_heads:
                z_ahead = scores(qh[hd + 1], key_blk, keys)
                suffixes_ahead = suffix_sums(softplus_keys(z_ahead, masks), widths)
            w = jnp.concatenate(weights_from(z, suffixes, hd, rows, widths, masks), axis=1)
            acc_ref[:rows, :] += _dot(w, jnp.where(v_head == hd, v, jnp.zeros_like(v)))
            if hd + 1 < n_heads:
                z, suffixes = z_ahead, suffixes_ahead

    def start_block(row0, rows):
        q = q_ref[pl.ds(row0, rows), :]
        q_head = head_of_lane(rows)
        acc_ref[...] = jnp.zeros_like(acc_ref)
        carry_ref[...] = jnp.zeros_like(carry_ref)
        return [jnp.where(q_head == hd, q, jnp.zeros_like(q)) for hd in range(n_heads)]

    def write(row0, rows):
        o_ref[pl.ds(row0, rows), :] = acc_ref[:rows, :].astype(o_ref.dtype)

    def weights_live():
        return (jnp.max(carry_ref[...]) > ATTN_LOG2_WEIGHT_FLOOR).astype(jnp.int32)

    wide = ATTN_SUFFIX_BLOCK
    group_widths = [wide] * (group * blk // wide)
    group_keys = group * blk
    near = wide // blk
    tail_blocks = 1 + group - near
    tail_widths = [blk] * (tail_blocks % near) + [wide] * (tail_blocks // near)
    assert group_keys == qrows and 0 < near <= group

    sweep(start_block(0, blk), blk, 0, [blk], 0)
    write(0, blk)
    sweep(start_block(blk, qrows), qrows, 0, [blk] + group_widths, blk)
    write(blk, qrows)

    def wide_block(qi, _):
        row0 = pl.multiple_of(blk + qi * qrows, blk)
        first = 1 + qi * group
        qh = start_block(row0, qrows)
        sweep(qh, qrows, first - near, [wide] + group_widths, wide)

        @pl.when(weights_live() > 0)
        def _():
            def item_start(g):
                return jnp.maximum(first - near - (g + 1) * group, 0)

            def pv(g):
                acc_ref[...] += _dot(w_ref[...], stacked_values(item_start(g), group_keys))

            for hd in range(n_heads):
                z_ref[hd] = scores(qh[hd], item_start(0), group_keys)
            w_ref[...] = jnp.zeros_like(w_ref)

            def key_group(state):
                g, _ = state
                pv(g - 1)
                no_mask = [None] * group
                suffixes = suffix_sums(softplus_keys(z_ref[0], no_mask), group_widths)
                for hd in range(n_heads):
                    ahead = (suffix_sums(softplus_keys(z_ref[hd + 1], no_mask), group_widths)
                             if hd + 1 < n_heads else None)
                    w_head = weights_from(z_ref[hd], suffixes, hd, qrows, group_widths, no_mask)
                    w_ref[:, hd * group_keys:(hd + 1) * group_keys] = jnp.concatenate(w_head, axis=1)
                    z_ref[hd] = scores(qh[hd], item_start(g + 1), group_keys)
                    suffixes = ahead
                return g + 1, weights_live()

            n_done, live = lax.while_loop(lambda state: jnp.logical_and(state[0] < qi - 1, state[1] > 0),
                                          key_group, (jnp.int32(0), jnp.int32(1)))
            pv(n_done - 1)

            @pl.when(live > 0)
            def _():
                sweep(qh, qrows, 0, tail_widths, None)

        write(row0, qrows)
        return 0

    lax.fori_loop(1, (q_ref.shape[0] - blk) // qrows, wide_block, 0)


def _suffix_matrix():
    j = jnp.arange(ATTN_SUFFIX_BLOCK)[:, None]
    s = jnp.arange(ATTN_SUFFIX_BLOCK)[None, :]
    return -(j >= s).astype(_BF16)


def _sb_attention(q, kt, v):
    b, tp, d = q.shape
    assert (tp - ATTN_BLOCK) % ATTN_Q_ROWS == 0 and (ATTN_Q_ROWS // ATTN_BLOCK) % ATTN_KEY_GROUP == 0
    lanes = ATTN_HEADS_PER_STEP * HEAD_DIM
    return pl.pallas_call(
        _attn_kernel,
        grid=(b, d // lanes),
        in_specs=[pl.BlockSpec((None, tp, lanes), lambda i, j: (i, 0, j)),
                  pl.BlockSpec((None, lanes, tp), lambda i, j: (i, j, 0)),
                  pl.BlockSpec((None, tp, lanes), lambda i, j: (i, 0, j)),
                  _resident((ATTN_SUFFIX_BLOCK, ATTN_SUFFIX_BLOCK))],
        out_specs=pl.BlockSpec((None, tp, lanes), lambda i, j: (i, 0, j)),
        out_shape=jax.ShapeDtypeStruct((b, tp, d), _BF16),
        scratch_shapes=[pltpu.VMEM((ATTN_Q_ROWS, lanes), _F32),
                        pltpu.VMEM((ATTN_HEADS_PER_STEP, ATTN_Q_ROWS, LANES), _F32),
                        pltpu.VMEM((ATTN_HEADS_PER_STEP, ATTN_Q_ROWS, ATTN_KEY_GROUP * ATTN_BLOCK), _F32),
                        pltpu.VMEM((ATTN_Q_ROWS, ATTN_HEADS_PER_STEP * ATTN_KEY_GROUP * ATTN_BLOCK), _BF16)],
        compiler_params=pltpu.CompilerParams(dimension_semantics=("parallel", "parallel"),
                                             vmem_limit_bytes=VMEM_LIMIT),
        name="sb_attn",
    )(q, kt, v, _suffix_matrix())


def _gelu_tanh(x):
    return 0.5 * x * (1.0 + jnp.tanh(math.sqrt(2.0 / math.pi) * (x + 0.044715 * (x * x * x))))


def _sigmoid(x):
    return 1.0 / (1.0 + jnp.exp(-x))


def _lru_stage(h_ref, g_ref, win_ref, cw_ref, cb_ref, wg_ref, brg_ref, big_ref, lam_ref, y_ref,
               rec_ref, a_ref, b_ref, state_ref):
    rows, d = h_ref.shape
    tail = SUBLANES

    hn = _rms_norm(h_ref[...], g_ref[...]).astype(_BF16)
    gate = _gelu_tanh(_dot(hn, win_ref[:, :d]))
    rec_ref[tail:, :] = _dot(hn, win_ref[:, d:])
    yield

    u = cb_ref[...] + cw_ref[CONV_WIDTH - 1:CONV_WIDTH, :] * rec_ref[tail:, :]
    for j in range(CONV_WIDTH - 1):
        shift = CONV_WIDTH - 1 - j
        u = u + cw_ref[j:j + 1, :] * rec_ref[tail - shift:tail - shift + rows, :]
    rec_ref[0:tail, :] = rec_ref[rows:rows + tail, :]
    yield

    neg_lam = -lam_ref[...]
    log_a_unit = -LRU_C * (jnp.maximum(neg_lam, 0.0) + jnp.log1p(jnp.exp(-jnp.abs(neg_lam))))
    ub = u.astype(_BF16)
    for n in range(LRU_BLOCKS):
        cols = slice(n * LRU_BLOCK_DIM, (n + 1) * LRU_BLOCK_DIM)
        ri = _dot(ub[:, cols], wg_ref[n])
        r = _sigmoid(ri[:, :LRU_BLOCK_DIM] + brg_ref[:, cols])
        i = _sigmoid(ri[:, LRU_BLOCK_DIM:] + big_ref[:, cols])
        log_a = log_a_unit[:, cols] * r
        a = jnp.exp(log_a)
        one_minus_a2 = -jnp.tanh(log_a) * (1.0 + a * a)
        mult = jnp.where(one_minus_a2 > 0.0, one_minus_a2 * lax.rsqrt(one_minus_a2), 0.0)
        a_ref[:, cols] = a
        b_ref[:, cols] = mult * (i * u[:, cols])
        yield

    sub = lax.broadcasted_iota(jnp.int32, (SUBLANES, d), 0)
    state = state_ref[...]
    for gidx in range(rows // SUBLANES):
        r0 = gidx * SUBLANES
        av = a_ref[r0:r0 + SUBLANES, :]
        bv = b_ref[r0:r0 + SUBLANES, :]
        for s in (1, 2, 4):
            keep = sub >= s
            bv = jnp.where(keep, av * pltpu.roll(bv, s, axis=0) + bv, bv)
            av = jnp.where(keep, av * pltpu.roll(av, s, axis=0), av)
        hv = av * state + bv
        b_ref[r0:r0 + SUBLANES, :] = hv
        state = jnp.broadcast_to(hv[SUBLANES - 1:SUBLANES, :], (SUBLANES, d))
        if gidx % LRU_SCAN_GROUPS_PER_PIECE == LRU_SCAN_GROUPS_PER_PIECE - 1:
            yield
    state_ref[...] = state
    y_ref[...] = (b_ref[...] * gate).astype(y_ref.dtype)


def _mlp_stage(y, h, wo_ref, g_ref, wup_ref, wdn_ref, gf_ref, o_ref, hn_ref, final_norm, ff_chunk):
    d_ff = wup_ref.shape[1]
    h1 = h + _dot(y, wo_ref[...])
    hn_ref[...] = _rms_norm(h1, g_ref[...]).astype(_BF16)
    o_ref[...] = h1
    yield
    for c in range(0, d_ff, ff_chunk):
        up = jnp.maximum(_dot(hn_ref[...], wup_ref[:, c:c + ff_chunk]), 0.0)
        up = (up * up).astype(_BF16)
        yield
        o_ref[...] += _dot(up, wdn_ref[c:c + ff_chunk, :])
        yield
    if final_norm:
        o_ref[...] = _rms_norm(o_ref[...], gf_ref[...])


def _interleave(*stages):
    live = list(stages)
    while live:
        for stage in list(live):
            if next(stage, StopIteration) is StopIteration:
                live.remove(stage)


def _mlp_kernel(y_ref, h_ref, wo_ref, g_ref, wup_ref, wdn_ref, gf_ref, o_ref, hn_ref, *, final_norm):
    _interleave(_mlp_stage(y_ref[...], h_ref[...], wo_ref, g_ref, wup_ref, wdn_ref, gf_ref, o_ref, hn_ref,
                           final_norm, MLP_FF_CHUNK))


def _proj_mlp(y, h, w_o, g, w_up, w_down, g_final, final_norm):
    m, d = h.shape
    d_ff = w_up.shape[1]
    row_spec = pl.BlockSpec((MLP_ROWS, d), lambda i: (i, 0))
    return pl.pallas_call(
        functools.partial(_mlp_kernel, final_norm=final_norm),
        grid=(m // MLP_ROWS,),
        in_specs=[row_spec, row_spec, _resident((d, d)), _resident((1, d)), _resident((d, d_ff)),
                  _resident((d_ff, d)), _resident((1, d))],
        out_specs=row_spec,
        out_shape=jax.ShapeDtypeStruct((m, d), _F32),
        scratch_shapes=[pltpu.VMEM((MLP_ROWS, d), _BF16)],
        compiler_params=pltpu.CompilerParams(dimension_semantics=("parallel",), vmem_limit_bytes=VMEM_LIMIT),
        name="proj_mlp",
    )(y, h, w_o, g, w_up, w_down, g_final)


def _lru_layer_kernel(hcur_ref, hprev_ref, g_ref, win_ref, cw_ref, cb_ref, wg_ref, brg_ref, big_ref, lam_ref,
                      wo_ref, gm_ref, wup_ref, wdn_ref, gf_ref, out_ref,
                      rec_ref, a_ref, b_ref, state_ref, y_ref, hn_ref, obuf_ref, *, steps_per_batch, n_blocks):
    s = pl.program_id(0)
    rows = hcur_ref.shape[0]
    cur, prev = s % 2, 1 - s % 2

    @pl.when(s == 0)
    def _():
        y_ref[...] = jnp.zeros_like(y_ref)
        obuf_ref[...] = jnp.zeros_like(obuf_ref)

    @pl.when(s % steps_per_batch == 0)
    def _():
        rec_ref[0:SUBLANES, :] = jnp.zeros((SUBLANES, rec_ref.shape[1]), _F32)
        state_ref[...] = jnp.zeros_like(state_ref)

    @pl.when(s <= n_blocks)
    def _():
        y_prev = y_ref[...]
        _interleave(
            _lru_stage(hcur_ref, g_ref, win_ref, cw_ref, cb_ref, wg_ref, brg_ref, big_ref, lam_ref, y_ref,
                       rec_ref, a_ref, b_ref, state_ref),
            _mlp_stage(y_prev, hprev_ref[...], wo_ref, gm_ref, wup_ref, wdn_ref, gf_ref,
                       obuf_ref.at[cur], hn_ref, True, LRU_LAYER_FF_CHUNK))

    out_ref[0:rows - N_META, :] = obuf_ref[prev, N_META:rows, :]
    out_ref[rows - N_META:rows, :] = obuf_ref[cur, 0:N_META, :]


def _lru_layer(h, batch, seq, g, w_in, conv_w, conv_b, w_gates, b_rg, b_ig, lam, w_out, g_mlp, w_up, w_down,
               g_final):
    m, d = h.shape
    d_ff = w_up.shape[1]
    n_blocks = m // TIME_BLOCK
    steps_per_batch = n_blocks // batch
    cur_spec = pl.BlockSpec((TIME_BLOCK, d), lambda s: (jnp.minimum(s, n_blocks - 1), 0))
    prev_spec = pl.BlockSpec((TIME_BLOCK, d), lambda s: (jnp.clip(s - 1, 0, n_blocks - 1), 0))

    def out_index(s):
        block = jnp.maximum(s - 2, 0)
        return block // steps_per_batch, block % steps_per_batch, 0

    return pl.pallas_call(
        functools.partial(_lru_layer_kernel, steps_per_batch=steps_per_batch, n_blocks=n_blocks),
        grid=(n_blocks + 2,),
        in_specs=[cur_spec, prev_spec, _resident((1, d)), _resident((d, 2 * d)), _resident((CONV_WIDTH, d)),
                  _resident((1, d)), _resident((LRU_BLOCKS, LRU_BLOCK_DIM, 2 * LRU_BLOCK_DIM)),
                  _resident((1, d)), _resident((1, d)), _resident((1, d)),
                  _resident((d, d)), _resident((1, d)), _resident((d, d_ff)), _resident((d_ff, d)),
                  _resident((1, d))],
        out_specs=pl.BlockSpec((None, TIME_BLOCK, d), out_index),
        out_shape=jax.ShapeDtypeStruct((batch, seq, d), _F32),
        scratch_shapes=[pltpu.VMEM((SUBLANES + TIME_BLOCK, d), _F32), pltpu.VMEM((TIME_BLOCK, d), _F32),
                        pltpu.VMEM((TIME_BLOCK, d), _F32), pltpu.VMEM((SUBLANES, d), _F32),
                        pltpu.VMEM((TIME_BLOCK, d), _BF16), pltpu.VMEM((TIME_BLOCK, d), _BF16),
                        pltpu.VMEM((2, TIME_BLOCK, d), _F32)],
        compiler_params=pltpu.CompilerParams(dimension_semantics=("arbitrary",), vmem_limit_bytes=VMEM_LIMIT),
        name="lru_layer",
    )(h, h, g, w_in, conv_w, conv_b, w_gates, b_rg, b_ig, lam, w_out, g_mlp, w_up, w_down, g_final)


def kernel(x, meta_tokens, norm_mix, norm_mlp, sb_w_qkv, sb_w_o, lru_w_in, lru_conv_w, lru_conv_b, lru_w_rg,
           lru_b_rg, lru_w_ig, lru_b_ig, lru_lambda, lru_w_out, mlp_w_up, mlp_w_down, norm_final):
    b, seq, d = x.shape
    assert d == HEADS * HEAD_DIM == LRU_BLOCKS * LRU_BLOCK_DIM
    t_len = N_META + seq
    tp = -(-t_len // TIME_BLOCK) * TIME_BLOCK
    assert (b * tp) % MLP_ROWS == 0 and tp % ATTN_BLOCK == 0

    row = lambda v: v.reshape(1, d)
    w_qkv = sb_w_qkv[0]
    q, kt, v, h = _qkv(x, meta_tokens.astype(x.dtype), tp, row(norm_mix[0]), w_qkv[:, :d].astype(_BF16),
                       w_qkv[:, d:2 * d].T.astype(_BF16), w_qkv[:, 2 * d:].astype(_BF16))
    o = _sb_attention(q, kt, v)
    h = _proj_mlp(o.reshape(b * tp, d), h.reshape(b * tp, d), sb_w_o[0].astype(_BF16), row(norm_mlp[0]),
                  mlp_w_up[0].astype(_BF16), mlp_w_down[0].astype(_BF16), row(norm_final), False)

    w_gates = jnp.concatenate([lru_w_rg[0], lru_w_ig[0]], axis=-1).astype(_BF16)
    return _lru_layer(h, b, seq, row(norm_mix[1]), lru_w_in[0].astype(_BF16), lru_conv_w[0],
                      row(lru_conv_b[0]), w_gates, row(lru_b_rg[0]), row(lru_b_ig[0]), row(lru_lambda[0]),
                      lru_w_out[0].astype(_BF16), row(norm_mlp[1]), mlp_w_up[1].astype(_BF16),
                      mlp_w_down[1].astype(_BF16), row(norm_final))
```

```python
import functools
import math

import jax
import jax.numpy as jnp
from jax import lax
from jax.experimental import pallas as pl
from jax.experimental.pallas import tpu as pltpu

N_META = 16
HEADS = 16
HEAD_DIM = 64
LRU_BLOCKS = 8
LRU_BLOCK_DIM = 128
CONV_WIDTH = 4
LRU_C = 8.0
EPS = 1e-6
LOG2E = 1.4426950408889634

LANES = 128
SUBLANES = 8
TIME_BLOCK = 384
ATTN_BLOCK = 128
ATTN_Q_ROWS = 256
ATTN_KEY_GROUP = 2
ATTN_SUFFIX_BLOCK = 256
ATTN_HEAD_GROUP = 4
ATTN_HEADS_PER_STEP = 8
ATTN_LOG2_WEIGHT_FLOOR = -192.0
MLP_ROWS = 512
MLP_FF_CHUNK = 1024
LRU_SCAN_GROUPS_PER_PIECE = 8
LRU_LAYER_FF_CHUNK = 512
VMEM_LIMIT = 56 * 1024 * 1024

_BF16 = jnp.bfloat16
_F32 = jnp.float32


def _dot(a, b):
    return jnp.dot(a, b, preferred_element_type=_F32)


def _rms_norm(x, g):
    ms = jnp.mean(x * x, axis=-1, keepdims=True)
    return (x * lax.rsqrt(ms + EPS)) * g


def _resident(shape):
    zeros = (0,) * len(shape)
    return pl.BlockSpec(shape, lambda *_: zeros, pipeline_mode=pl.Buffered(1))


def _qkv_kernel(x_ref, meta_ref, g_ref, wq_ref, wkt_ref, wv_ref, q_ref, kt_ref, v_ref, h_ref, *, pad_rows):
    j, last = pl.program_id(1), pl.num_programs(1) - 1
    rows, d = x_ref.shape
    x = x_ref[...]
    first_block = jnp.concatenate([meta_ref[...], x[:rows - N_META]], axis=0)
    last_block = jnp.concatenate([x[pad_rows:], jnp.zeros((pad_rows, d), x.dtype)], axis=0)
    h = jnp.where(j == 0, first_block, jnp.where(j == last, last_block, x))
    h_ref[...] = h
    hn = _rms_norm(h, g_ref[...]).astype(_BF16)
    q_ref[...] = (_dot(hn, wq_ref[...]) * (HEAD_DIM ** -0.5 * LOG2E)).astype(_BF16)
    v_ref[...] = _dot(hn, wv_ref[...]).astype(_BF16)
    kt = lax.dot_general(wkt_ref[...], hn, (((1,), (1,)), ((), ())), preferred_element_type=_F32)
    kt_ref[...] = kt.astype(_BF16)


def _qkv(x, meta, tp, g, wq, wkt, wv):
    b, seq, d = x.shape
    nt = tp // TIME_BLOCK
    pad_rows = tp - N_META - seq
    assert nt >= 2 and seq >= TIME_BLOCK and all(n % SUBLANES == 0 for n in (pad_rows, N_META, seq, TIME_BLOCK))
    row_spec = pl.BlockSpec((None, TIME_BLOCK, d), lambda i, j: (i, j, 0))
    tiles, meta_tiles = TIME_BLOCK // SUBLANES, N_META // SUBLANES
    x_spec = pl.BlockSpec(
        (pl.Squeezed(), pl.Element(TIME_BLOCK), pl.Element(d)),
        lambda i, j: (i, jnp.clip(j * tiles - meta_tiles, 0, (seq - TIME_BLOCK) // SUBLANES) * SUBLANES, 0))
    return pl.pallas_call(
        functools.partial(_qkv_kernel, pad_rows=pad_rows),
        grid=(b, nt),
        in_specs=[x_spec, _resident((N_META, d)), _resident((1, d)), _resident((d, d)), _resident((d, d)),
                  _resident((d, d))],
        out_specs=[row_spec, pl.BlockSpec((None, d, TIME_BLOCK), lambda i, j: (i, 0, j)), row_spec, row_spec],
        out_shape=[jax.ShapeDtypeStruct((b, tp, d), _BF16), jax.ShapeDtypeStruct((b, d, tp), _BF16),
                   jax.ShapeDtypeStruct((b, tp, d), _BF16), jax.ShapeDtypeStruct((b, tp, d), _F32)],
        compiler_params=pltpu.CompilerParams(dimension_semantics=("parallel", "parallel"),
                                             vmem_limit_bytes=VMEM_LIMIT),
        name="qkv",
    )(x, meta, g, wq, wkt, wv)


def _attn_kernel(q_ref, kt_ref, v_ref, nu_ref, o_ref, acc_ref, carry_ref, z_ref, w_ref):
    blk, qrows, group = ATTN_BLOCK, ATTN_Q_ROWS, ATTN_KEY_GROUP
    n_heads = q_ref.shape[1] // HEAD_DIM
    lanes = ATTN_HEAD_GROUP * HEAD_DIM

    def group_lanes(hd):
        lo = hd // ATTN_HEAD_GROUP * lanes
        return slice(lo, lo + lanes)

    def head_of_lane(rows):
        return lax.broadcasted_iota(jnp.int32, (rows, lanes), 1) // HEAD_DIM

    def stacked_values(key_blk, keys, first_head):
        v = v_ref[pl.ds(pl.multiple_of(key_blk * blk, blk), keys), group_lanes(first_head)]
        v_head = head_of_lane(keys)
        return jnp.concatenate([jnp.where(v_head == hd, v, jnp.zeros_like(v)) for hd in range(ATTN_HEAD_GROUP)],
                               axis=0)

    def scores(q_head, hd, key_blk, keys):
        return _dot(q_head, kt_ref[group_lanes(hd), pl.ds(pl.multiple_of(key_blk * blk, blk), keys)])

    def visibility(rows, n_blk, causal_shift):
        masks = []
        for k_idx in range(n_blk):
            if causal_shift is None or (k_idx + 1) * blk - causal_shift <= 0:
                masks.append(None)
            else:
                key_pos = lax.broadcasted_iota(jnp.int32, (rows, blk), 1) + (k_idx * blk - causal_shift)
                masks.append(key_pos < lax.broadcasted_iota(jnp.int32, (rows, blk), 0))
        return masks

    def softplus_keys(z, masks):
        sp = jnp.maximum(jnp.log2(1.0 + jnp.exp2(jnp.minimum(z, 100.0))), z)
        return jnp.concatenate([sp[:, k * blk:(k + 1) * blk] if m is None else
                                jnp.where(m, sp[:, k * blk:(k + 1) * blk], 0.0)
                                for k, m in enumerate(masks)], axis=1).astype(_BF16)

    def suffix_sums(sp, widths):
        out, lo = [], 0
        for width in widths:
            out.append(_dot(sp[:, lo:lo + width], nu_ref[:width, :width]))
            lo += width
        return out

    def weights_from(z, suffixes, hd, rows, widths, masks):
        carry = carry_ref[hd, :rows, :]
        w_head = [None] * len(masks)
        lo = sum(widths)
        for width, suffix in zip(reversed(widths), reversed(suffixes)):
            lo -= width
            for c in range(width // blk):
                k_idx = lo // blk + c
                w = jnp.exp2(z[:, k_idx * blk:(k_idx + 1) * blk] + suffix[:, c * blk:(c + 1) * blk] + carry)
                if masks[k_idx] is not None:
                    w = jnp.where(masks[k_idx], w, 0.0)
                w_head[k_idx] = w.astype(_BF16)
            carry = carry + jnp.broadcast_to(suffix[:, 0:1], (rows, LANES))
        carry_ref[hd, :rows, :] = carry
        return w_head

    def sweep(qh, rows, key_blk, widths, causal_shift):
        keys = sum(widths)
        masks = visibility(rows, keys // blk, causal_shift)
        v = v_ref[pl.ds(pl.multiple_of(key_blk * blk, blk), keys), :]
        v_head = head_of_lane(keys)
        z = scores(qh[0], 0, key_blk, keys)
        suffixes = suffix_sums(softplus_keys(z, masks), widths)
        for hd in range(n_heads):
            if hd + 1 < n_heads:
                z_ahead = scores(qh[hd + 1], hd + 1, key_blk, keys)
                suffixes_ahead = suffix_sums(softplus_keys(z_ahead, masks), widths)
            w = jnp.concatenate(weights_from(z, suffixes, hd, rows, widths, masks), axis=1)
            v_group = v[:, group_lanes(hd)]
            acc_ref[:rows, group_lanes(hd)] += _dot(
                w, jnp.where(v_head == hd % ATTN_HEAD_GROUP, v_group, jnp.zeros_like(v_group)))
            if hd + 1 < n_heads:
                z, suffixes = z_ahead, suffixes_ahead

    def start_block(row0, rows):
        q = q_ref[pl.ds(row0, rows), :]
        q_head = head_of_lane(rows)
        acc_ref[...] = jnp.zeros_like(acc_ref)
        carry_ref[...] = jnp.zeros_like(carry_ref)
        groups = [q[:, group_lanes(hd)] for hd in range(0, n_heads, ATTN_HEAD_GROUP)]
        return [jnp.where(q_head == hd % ATTN_HEAD_GROUP, groups[hd // ATTN_HEAD_GROUP],
                          jnp.zeros_like(groups[0])) for hd in range(n_heads)]

    def write(row0, rows):
        o_ref[pl.ds(row0, rows), :] = acc_ref[:rows, :].astype(o_ref.dtype)

    def weights_live():
        return (jnp.max(carry_ref[...]) > ATTN_LOG2_WEIGHT_FLOOR).astype(jnp.int32)

    wide = ATTN_SUFFIX_BLOCK
    group_widths = [wide] * (group * blk // wide)
    group_keys = group * blk
    near = wide // blk
    tail_blocks = 1 + group - near
    tail_widths = [blk] * (tail_blocks % near) + [wide] * (tail_blocks // near)
    assert group_keys == qrows and 0 < near <= group

    sweep(start_block(0, blk), blk, 0, [blk], 0)
    write(0, blk)
    sweep(start_block(blk, qrows), qrows, 0, [blk] + group_widths, blk)
    write(blk, qrows)

    def wide_block(qi, _):
        row0 = pl.multiple_of(blk + qi * qrows, blk)
        first = 1 + qi * group
        qh = start_block(row0, qrows)
        sweep(qh, qrows, first - near, [wide] + group_widths, wide)

        @pl.when(weights_live() > 0)
        def _():
            def item_start(g):
                return jnp.maximum(first - near - (g + 1) * group, 0)

            def pv(g):
                for hd in range(0, n_heads, ATTN_HEAD_GROUP):
                    cols = slice(hd * group_keys, (hd + ATTN_HEAD_GROUP) * group_keys)
                    acc_ref[:, group_lanes(hd)] += _dot(w_ref[:, cols],
                                                        stacked_values(item_start(g), group_keys, hd))

            for hd in range(n_heads):
                z_ref[hd] = scores(qh[hd], hd, item_start(0), group_keys)
            w_ref[...] = jnp.zeros_like(w_ref)

            def key_group(state):
                g, _ = state
                pv(g - 1)
                no_mask = [None] * group
                suffixes = suffix_sums(softplus_keys(z_ref[0], no_mask), group_widths)
                for hd in range(n_heads):
                    ahead = (suffix_sums(softplus_keys(z_ref[hd + 1], no_mask), group_widths)
                             if hd + 1 < n_heads else None)
                    w_head = weights_from(z_ref[hd], suffixes, hd, qrows, group_widths, no_mask)
                    w_ref[:, hd * group_keys:(hd + 1) * group_keys] = jnp.concatenate(w_head, axis=1)
                    z_ref[hd] = scores(qh[hd], hd, item_start(g + 1), group_keys)
                    suffixes = ahead
                return g + 1, weights_live()

            n_done, live = lax.while_loop(lambda state: jnp.logical_and(state[0] < qi - 1, state[1] > 0),
                                          key_group, (jnp.int32(0), jnp.int32(1)))
            pv(n_done - 1)

            @pl.when(live > 0)
            def _():
                sweep(qh, qrows, 0, tail_widths, None)

        write(row0, qrows)
        return 0

    lax.fori_loop(1, (q_ref.shape[0] - blk) // qrows, wide_block, 0)


def _suffix_matrix():
    j = jnp.arange(ATTN_SUFFIX_BLOCK)[:, None]
    s = jnp.arange(ATTN_SUFFIX_BLOCK)[None, :]
    return -(j >= s).astype(_BF16)


def _sb_attention(q, kt, v):
    b, tp, d = q.shape
    assert (tp - ATTN_BLOCK) % ATTN_Q_ROWS == 0 and (ATTN_Q_ROWS // ATTN_BLOCK) % ATTN_KEY_GROUP == 0
    lanes = ATTN_HEADS_PER_STEP * HEAD_DIM
    return pl.pallas_call(
        _attn_kernel,
        grid=(b, d // lanes),
        in_specs=[pl.BlockSpec((None, tp, lanes), lambda i, j: (i, 0, j)),
                  pl.BlockSpec((None, lanes, tp), lambda i, j: (i, j, 0)),
                  pl.BlockSpec((None, tp, lanes), lambda i, j: (i, 0, j)),
                  _resident((ATTN_SUFFIX_BLOCK, ATTN_SUFFIX_BLOCK))],
        out_specs=pl.BlockSpec((None, tp, lanes), lambda i, j: (i, 0, j)),
        out_shape=jax.ShapeDtypeStruct((b, tp, d), _BF16),
        scratch_shapes=[pltpu.VMEM((ATTN_Q_ROWS, lanes), _F32),
                        pltpu.VMEM((ATTN_HEADS_PER_STEP, ATTN_Q_ROWS, LANES), _F32),
                        pltpu.VMEM((ATTN_HEADS_PER_STEP, ATTN_Q_ROWS, ATTN_KEY_GROUP * ATTN_BLOCK), _F32),
                        pltpu.VMEM((ATTN_Q_ROWS, ATTN_HEADS_PER_STEP * ATTN_KEY_GROUP * ATTN_BLOCK), _BF16)],
        compiler_params=pltpu.CompilerParams(dimension_semantics=("parallel", "parallel"),
                                             vmem_limit_bytes=VMEM_LIMIT),
        name="sb_attn",
    )(q, kt, v, _suffix_matrix())


def _gelu_tanh(x):
    return 0.5 * x * (1.0 + jnp.tanh(math.sqrt(2.0 / math.pi) * (x + 0.044715 * (x * x * x))))


def _sigmoid(x):
    return 1.0 / (1.0 + jnp.exp(-x))


def _lru_stage(h_ref, g_ref, win_ref, cw_ref, cb_ref, wg_ref, brg_ref, big_ref, lam_ref, y_ref,
               rec_ref, a_ref, b_ref, state_ref):
    rows, d = h_ref.shape
    tail = SUBLANES

    hn = _rms_norm(h_ref[...], g_ref[...]).astype(_BF16)
    gate = _gelu_tanh(_dot(hn, win_ref[:, :d]))
    rec_ref[tail:, :] = _dot(hn, win_ref[:, d:])
    yield

    u = cb_ref[...] + cw_ref[CONV_WIDTH - 1:CONV_WIDTH, :] * rec_ref[tail:, :]
    for j in range(CONV_WIDTH - 1):
        shift = CONV_WIDTH - 1 - j
        u = u + cw_ref[j:j + 1, :] * rec_ref[tail - shift:tail - shift + rows, :]
    rec_ref[0:tail, :] = rec_ref[rows:rows + tail, :]
    yield

    neg_lam = -lam_ref[...]
    log_a_unit = -LRU_C * (jnp.maximum(neg_lam, 0.0) + jnp.log1p(jnp.exp(-jnp.abs(neg_lam))))
    ub = u.astype(_BF16)
    for n in range(LRU_BLOCKS):
        cols = slice(n * LRU_BLOCK_DIM, (n + 1) * LRU_BLOCK_DIM)
        ri = _dot(ub[:, cols], wg_ref[n])
        r = _sigmoid(ri[:, :LRU_BLOCK_DIM] + brg_ref[:, cols])
        i = _sigmoid(ri[:, LRU_BLOCK_DIM:] + big_ref[:, cols])
        log_a = log_a_unit[:, cols] * r
        a = jnp.exp(log_a)
        one_minus_a2 = -jnp.tanh(log_a) * (1.0 + a * a)
        mult = jnp.where(one_minus_a2 > 0.0, one_minus_a2 * lax.rsqrt(one_minus_a2), 0.0)
        a_ref[:, cols] = a
        b_ref[:, cols] = mult * (i * u[:, cols])
        yield

    sub = lax.broadcasted_iota(jnp.int32, (SUBLANES, d), 0)
    state = state_ref[...]
    for gidx in range(rows // SUBLANES):
        r0 = gidx * SUBLANES
        av = a_ref[r0:r0 + SUBLANES, :]
        bv = b_ref[r0:r0 + SUBLANES, :]
        for s in (1, 2, 4):
            keep = sub >= s
            bv = jnp.where(keep, av * pltpu.roll(bv, s, axis=0) + bv, bv)
            av = jnp.where(keep, av * pltpu.roll(av, s, axis=0), av)
        hv = av * state + bv
        b_ref[r0:r0 + SUBLANES, :] = hv
        state = jnp.broadcast_to(hv[SUBLANES - 1:SUBLANES, :], (SUBLANES, d))
        if gidx % LRU_SCAN_GROUPS_PER_PIECE == LRU_SCAN_GROUPS_PER_PIECE - 1:
            yield
    state_ref[...] = state
    y_ref[...] = (b_ref[...] * gate).astype(y_ref.dtype)


def _mlp_stage(y, h, wo_ref, g_ref, wup_ref, wdn_ref, gf_ref, o_ref, hn_ref, final_norm, ff_chunk):
    d_ff = wup_ref.shape[1]
    h1 = h + _dot(y, wo_ref[...])
    hn_ref[...] = _rms_norm(h1, g_ref[...]).astype(_BF16)
    o_ref[...] = h1
    yield
    for c in range(0, d_ff, ff_chunk):
        up = jnp.maximum(_dot(hn_ref[...], wup_ref[:, c:c + ff_chunk]), 0.0)
        up = (up * up).astype(_BF16)
        yield
        o_ref[...] += _dot(up, wdn_ref[c:c + ff_chunk, :])
        yield
    if final_norm:
        o_ref[...] = _rms_norm(o_ref[...], gf_ref[...])


def _interleave(*stages):
    live = list(stages)
    while live:
        for stage in list(live):
            if next(stage, StopIteration) is StopIteration:
                live.remove(stage)


def _mlp_kernel(y_ref, h_ref, wo_ref, g_ref, wup_ref, wdn_ref, gf_ref, o_ref, hn_ref, *, final_norm):
    _interleave(_mlp_stage(y_ref[...], h_ref[...], wo_ref, g_ref, wup_ref, wdn_ref, gf_ref, o_ref, hn_ref,
                           final_norm, MLP_FF_CHUNK))


def _proj_mlp(y, h, w_o, g, w_up, w_down, g_final, final_norm):
    m, d = h.shape
    d_ff = w_up.shape[1]
    row_spec = pl.BlockSpec((MLP_ROWS, d), lambda i: (i, 0))
    return pl.pallas_call(
        functools.partial(_mlp_kernel, final_norm=final_norm),
        grid=(m // MLP_ROWS,),
        in_specs=[row_spec, row_spec, _resident((d, d)), _resident((1, d)), _resident((d, d_ff)),
                  _resident((d_ff, d)), _resident((1, d))],
        out_specs=row_spec,
        out_shape=jax.ShapeDtypeStruct((m, d), _F32),
        scratch_shapes=[pltpu.VMEM((MLP_ROWS, d), _BF16)],
        compiler_params=pltpu.CompilerParams(dimension_semantics=("parallel",), vmem_limit_bytes=VMEM_LIMIT),
        name="proj_mlp",
    )(y, h, w_o, g, w_up, w_down, g_final)


def _lru_layer_kernel(hcur_ref, hprev_ref, g_ref, win_ref, cw_ref, cb_ref, wg_ref, brg_ref, big_ref, lam_ref,
                      wo_ref, gm_ref, wup_ref, wdn_ref, gf_ref, out_ref,
                      rec_ref, a_ref, b_ref, state_ref, y_ref, hn_ref, obuf_ref, *, steps_per_batch, n_blocks):
    s = pl.program_id(0)
    rows = hcur_ref.shape[0]
    cur, prev = s % 2, 1 - s % 2

    @pl.when(s == 0)
    def _():
        y_ref[...] = jnp.zeros_like(y_ref)
        obuf_ref[...] = jnp.zeros_like(obuf_ref)

    @pl.when(s % steps_per_batch == 0)
    def _():
        rec_ref[0:SUBLANES, :] = jnp.zeros((SUBLANES, rec_ref.shape[1]), _F32)
        state_ref[...] = jnp.zeros_like(state_ref)

    @pl.when(s <= n_blocks)
    def _():
        y_prev = y_ref[...]
        _interleave(
            _lru_stage(hcur_ref, g_ref, win_ref, cw_ref, cb_ref, wg_ref, brg_ref, big_ref, lam_ref, y_ref,
                       rec_ref, a_ref, b_ref, state_ref),
            _mlp_stage(y_prev, hprev_ref[...], wo_ref, gm_ref, wup_ref, wdn_ref, gf_ref,
                       obuf_ref.at[cur], hn_ref, True, LRU_LAYER_FF_CHUNK))

    out_ref[0:rows - N_META, :] = obuf_ref[prev, N_META:rows, :]
    out_ref[rows - N_META:rows, :] = obuf_ref[cur, 0:N_META, :]


def _lru_layer(h, batch, seq, g, w_in, conv_w, conv_b, w_gates, b_rg, b_ig, lam, w_out, g_mlp, w_up, w_down,
               g_final):
    m, d = h.shape
    d_ff = w_up.shape[1]
    n_blocks = m // TIME_BLOCK
    steps_per_batch = n_blocks // batch
    cur_spec = pl.BlockSpec((TIME_BLOCK, d), lambda s: (jnp.minimum(s, n_blocks - 1), 0))
    prev_spec = pl.BlockSpec((TIME_BLOCK, d), lambda s: (jnp.clip(s - 1, 0, n_blocks - 1), 0))

    def out_index(s):
        block = jnp.maximum(s - 2, 0)
        return block // steps_per_batch, block % steps_per_batch, 0

    return pl.pallas_call(
        functools.partial(_lru_layer_kernel, steps_per_batch=steps_per_batch, n_blocks=n_blocks),
        grid=(n_blocks + 2,),
        in_specs=[cur_spec, prev_spec, _resident((1, d)), _resident((d, 2 * d)), _resident((CONV_WIDTH, d)),
                  _resident((1, d)), _resident((LRU_BLOCKS, LRU_BLOCK_DIM, 2 * LRU_BLOCK_DIM)),
                  _resident((1, d)), _resident((1, d)), _resident((1, d)),
                  _resident((d, d)), _resident((1, d)), _resident((d, d_ff)), _resident((d_ff, d)),
                  _resident((1, d))],
        out_specs=pl.BlockSpec((None, TIME_BLOCK, d), out_index),
        out_shape=jax.ShapeDtypeStruct((batch, seq, d), _F32),
        scratch_shapes=[pltpu.VMEM((SUBLANES + TIME_BLOCK, d), _F32), pltpu.VMEM((TIME_BLOCK, d), _F32),
                        pltpu.VMEM((TIME_BLOCK, d), _F32), pltpu.VMEM((SUBLANES, d), _F32),
                        pltpu.VMEM((TIME_BLOCK, d), _BF16), pltpu.VMEM((TIME_BLOCK, d), _BF16),
                        pltpu.VMEM((2, TIME_BLOCK, d), _F32)],
        compiler_params=pltpu.CompilerParams(dimension_semantics=("arbitrary",), vmem_limit_bytes=VMEM_LIMIT),
        name="lru_layer",
    )(h, h, g, w_in, conv_w, conv_b, w_gates, b_rg, b_ig, lam, w_out, g_mlp, w_up, w_down, g_final)


def kernel(x, meta_tokens, norm_mix, norm_mlp, sb_w_qkv, sb_w_o, lru_w_in, lru_conv_w, lru_conv_b, lru_w_rg,
           lru_b_rg, lru_w_ig, lru_b_ig, lru_lambda, lru_w_out, mlp_w_up, mlp_w_down, norm_final):
    b, seq, d = x.shape
    assert d == HEADS * HEAD_DIM == LRU_BLOCKS * LRU_BLOCK_DIM
    t_len = N_META + seq
    tp = -(-t_len // TIME_BLOCK) * TIME_BLOCK
    assert (b * tp) % MLP_ROWS == 0 and tp % ATTN_BLOCK == 0

    row = lambda v: v.reshape(1, d)
    w_qkv = sb_w_qkv[0]
    q, kt, v, h = _qkv(x, meta_tokens.astype(x.dtype), tp, row(norm_mix[0]), w_qkv[:, :d].astype(_BF16),
                       w_qkv[:, d:2 * d].T.astype(_BF16), w_qkv[:, 2 * d:].astype(_BF16))
    o = _sb_attention(q, kt, v)
    h = _proj_mlp(o.reshape(b * tp, d), h.reshape(b * tp, d), sb_w_o[0].astype(_BF16), row(norm_mlp[0]),
                  mlp_w_up[0].astype(_BF16), mlp_w_down[0].astype(_BF16), row(norm_final), False)

    w_gates = jnp.concatenate([lru_w_rg[0], lru_w_ig[0]], axis=-1).astype(_BF16)
    return _lru_layer(h, b, seq, row(norm_mix[1]), lru_w_in[0].astype(_BF16), lru_conv_w[0],
                      row(lru_conv_b[0]), w_gates, row(lru_b_rg[0]), row(lru_b_ig[0]), row(lru_lambda[0]),
                      lru_w_out[0].astype(_BF16), row(norm_mlp[1]), mlp_w_up[1].astype(_BF16),
                      mlp_w_down[1].astype(_BF16), row(norm_final))
```

```python
import functools
import math

import jax
import jax.numpy as jnp
from jax import lax
from jax.experimental import pallas as pl
from jax.experimental.pallas import tpu as pltpu

N_META = 16
HEADS = 16
HEAD_DIM = 64
LRU_BLOCKS = 8
LRU_BLOCK_DIM = 128
CONV_WIDTH = 4
LRU_C = 8.0
EPS = 1e-6
LOG2E = 1.4426950408889634

LANES = 128
SUBLANES = 8
TIME_BLOCK = 384
ATTN_BLOCK = 128
ATTN_Q_ROWS = 256
ATTN_KEY_GROUP = 2
ATTN_SUFFIX_BLOCK = 256
ATTN_HEAD_GROUP = 4
ATTN_HEADS_PER_STEP = 8
ATTN_LOG2_WEIGHT_FLOOR = -192.0
MLP_ROWS = 512
MLP_FF_CHUNK = 1024
LRU_SCAN_GROUPS_PER_PIECE = 8
LRU_LAYER_FF_CHUNK = 512
VMEM_LIMIT = 56 * 1024 * 1024

_BF16 = jnp.bfloat16
_F32 = jnp.float32


def _dot(a, b):
    return jnp.dot(a, b, preferred_element_type=_F32)


def _rms_norm(x, g):
    ms = jnp.mean(x * x, axis=-1, keepdims=True)
    return (x * lax.rsqrt(ms + EPS)) * g


def _resident(shape):
    zeros = (0,) * len(shape)
    return pl.BlockSpec(shape, lambda *_: zeros, pipeline_mode=pl.Buffered(1))


def _qkv_kernel(x_ref, meta_ref, g_ref, wq_ref, wkt_ref, wv_ref, q_ref, kt_ref, v_ref, h_ref, *, pad_rows):
    j, last = pl.program_id(1), pl.num_programs(1) - 1
    rows, d = x_ref.shape
    x = x_ref[...]
    first_block = jnp.concatenate([meta_ref[...], x[:rows - N_META]], axis=0)
    last_block = jnp.concatenate([x[pad_rows:], jnp.zeros((pad_rows, d), x.dtype)], axis=0)
    h = jnp.where(j == 0, first_block, jnp.where(j == last, last_block, x))
    h_ref[...] = h
    hn = _rms_norm(h, g_ref[...]).astype(_BF16)
    q_ref[...] = (_dot(hn, wq_ref[...]) * (HEAD_DIM ** -0.5 * LOG2E)).astype(_BF16)
    v_ref[...] = _dot(hn, wv_ref[...]).astype(_BF16)
    kt = lax.dot_general(wkt_ref[...], hn, (((1,), (1,)), ((), ())), preferred_element_type=_F32)
    kt_ref[...] = kt.astype(_BF16)


def _qkv(x, meta, tp, g, wq, wkt, wv):
    b, seq, d = x.shape
    nt = tp // TIME_BLOCK
    pad_rows = tp - N_META - seq
    assert nt >= 2 and seq >= TIME_BLOCK and all(n % SUBLANES == 0 for n in (pad_rows, N_META, seq, TIME_BLOCK))
    row_spec = pl.BlockSpec((None, TIME_BLOCK, d), lambda i, j: (i, j, 0))
    tiles, meta_tiles = TIME_BLOCK // SUBLANES, N_META // SUBLANES
    x_spec = pl.BlockSpec(
        (pl.Squeezed(), pl.Element(TIME_BLOCK), pl.Element(d)),
        lambda i, j: (i, jnp.clip(j * tiles - meta_tiles, 0, (seq - TIME_BLOCK) // SUBLANES) * SUBLANES, 0))
    return pl.pallas_call(
        functools.partial(_qkv_kernel, pad_rows=pad_rows),
        grid=(b, nt),
        in_specs=[x_spec, _resident((N_META, d)), _resident((1, d)), _resident((d, d)), _resident((d, d)),
                  _resident((d, d))],
        out_specs=[row_spec, pl.BlockSpec((None, d, TIME_BLOCK), lambda i, j: (i, 0, j)), row_spec, row_spec],
        out_shape=[jax.ShapeDtypeStruct((b, tp, d), _BF16), jax.ShapeDtypeStruct((b, d, tp), _BF16),
                   jax.ShapeDtypeStruct((b, tp, d), _BF16), jax.ShapeDtypeStruct((b, tp, d), _F32)],
        compiler_params=pltpu.CompilerParams(dimension_semantics=("parallel", "parallel"),
                                             vmem_limit_bytes=VMEM_LIMIT),
        name="qkv",
    )(x, meta, g, wq, wkt, wv)


def _attn_kernel(q_ref, kt_ref, v_ref, nu_ref, o_ref, acc_ref, carry_ref, z_ref, w_ref):
    blk, qrows, group = ATTN_BLOCK, ATTN_Q_ROWS, ATTN_KEY_GROUP
    n_heads = q_ref.shape[1] // HEAD_DIM
    lanes = ATTN_HEAD_GROUP * HEAD_DIM

    def group_lanes(hd):
        lo = hd // ATTN_HEAD_GROUP * lanes
        return slice(lo, lo + lanes)

    def head_of_lane(rows):
        return lax.broadcasted_iota(jnp.int32, (rows, lanes), 1) // HEAD_DIM

    def stacked_values(key_blk, keys, first_head):
        v = v_ref[pl.ds(pl.multiple_of(key_blk * blk, blk), keys), group_lanes(first_head)]
        v_head = head_of_lane(keys)
        return jnp.concatenate([jnp.where(v_head == hd, v, jnp.zeros_like(v)) for hd in range(ATTN_HEAD_GROUP)],
                               axis=0)

    def scores(q_head, hd, key_blk, keys):
        return _dot(q_head, kt_ref[group_lanes(hd), pl.ds(pl.multiple_of(key_blk * blk, blk), keys)])

    def visibility(rows, n_blk, causal_shift):
        masks = []
        for k_idx in range(n_blk):
            if causal_shift is None or (k_idx + 1) * blk - causal_shift <= 0:
                masks.append(None)
            else:
                key_pos = lax.broadcasted_iota(jnp.int32, (rows, blk), 1) + (k_idx * blk - causal_shift)
                masks.append(key_pos < lax.broadcasted_iota(jnp.int32, (rows, blk), 0))
        return masks

    def softplus_keys(z, masks):
        sp = jnp.maximum(jnp.log2(1.0 + jnp.exp2(jnp.minimum(z, 100.0))), z)
        return jnp.concatenate([sp[:, k * blk:(k + 1) * blk] if m is None else
                                jnp.where(m, sp[:, k * blk:(k + 1) * blk], 0.0)
                                for k, m in enumerate(masks)], axis=1).astype(_BF16)

    def suffix_sums(sp, widths):
        out, lo = [], 0
        for width in widths:
            out.append(_dot(sp[:, lo:lo + width], nu_ref[:width, :width]))
            lo += width
        return out

    def weights_from(z, suffixes, hd, rows, widths, masks):
        carry = carry_ref[hd, :rows, :]
        w_head = [None] * len(masks)
        lo = sum(widths)
        for width, suffix in zip(reversed(widths), reversed(suffixes)):
            lo -= width
            for c in range(width // blk):
                k_idx = lo // blk + c
                w = jnp.exp2(z[:, k_idx * blk:(k_idx + 1) * blk] + suffix[:, c * blk:(c + 1) * blk] + carry)
                if masks[k_idx] is not None:
                    w = jnp.where(masks[k_idx], w, 0.0)
                w_head[k_idx] = w.astype(_BF16)
            carry = carry + jnp.broadcast_to(suffix[:, 0:1], (rows, LANES))
        carry_ref[hd, :rows, :] = carry
        return w_head

    def sweep(qh, rows, key_blk, widths, causal_shift):
        keys = sum(widths)
        masks = visibility(rows, keys // blk, causal_shift)
        v = v_ref[pl.ds(pl.multiple_of(key_blk * blk, blk), keys), :]
        v_head = head_of_lane(keys)
        zs, suffixes = {}, {}
        for tick in range(n_heads + 2):
            if tick < n_heads:
                zs[tick] = scores(qh[tick], tick, key_blk, keys)
            if 0 <= tick - 1 < n_heads:
                suffixes[tick - 1] = suffix_sums(softplus_keys(zs[tick - 1], masks), widths)
            if 0 <= tick - 2 < n_heads:
                hd = tick - 2
                w = jnp.concatenate(weights_from(zs.pop(hd), suffixes.pop(hd), hd, rows, widths, masks), axis=1)
                v_group = v[:, group_lanes(hd)]
                acc_ref[:rows, group_lanes(hd)] += _dot(
                    w, jnp.where(v_head == hd % ATTN_HEAD_GROUP, v_group, jnp.zeros_like(v_group)))

    def start_block(row0, rows):
        q = q_ref[pl.ds(row0, rows), :]
        q_head = head_of_lane(rows)
        acc_ref[...] = jnp.zeros_like(acc_ref)
        carry_ref[...] = jnp.zeros_like(carry_ref)
        groups = [q[:, group_lanes(hd)] for hd in range(0, n_heads, ATTN_HEAD_GROUP)]
        return [jnp.where(q_head == hd % ATTN_HEAD_GROUP, groups[hd // ATTN_HEAD_GROUP],
                          jnp.zeros_like(groups[0])) for hd in range(n_heads)]

    def write(row0, rows):
        o_ref[pl.ds(row0, rows), :] = acc_ref[:rows, :].astype(o_ref.dtype)

    def weights_live():
        return (jnp.max(carry_ref[...]) > ATTN_LOG2_WEIGHT_FLOOR).astype(jnp.int32)

    wide = ATTN_SUFFIX_BLOCK
    group_widths = [wide] * (group * blk // wide)
    group_keys = group * blk
    near = wide // blk
    tail_blocks = 1 + group - near
    tail_widths = [blk] * (tail_blocks % near) + [wide] * (tail_blocks // near)
    assert group_keys == qrows and 0 < near <= group

    sweep(start_block(0, blk), blk, 0, [blk], 0)
    write(0, blk)
    sweep(start_block(blk, qrows), qrows, 0, [blk] + group_widths, blk)
    write(blk, qrows)

    def wide_block(qi, _):
        row0 = pl.multiple_of(blk + qi * qrows, blk)
        first = 1 + qi * group
        qh = start_block(row0, qrows)
        sweep(qh, qrows, first - near, [wide] + group_widths, wide)

        @pl.when(weights_live() > 0)
        def _():
            def item_start(g):
                return jnp.maximum(first - near - (g + 1) * group, 0)

            def pv(g):
                for hd in range(0, n_heads, ATTN_HEAD_GROUP):
                    cols = slice(hd * group_keys, (hd + ATTN_HEAD_GROUP) * group_keys)
                    acc_ref[:, group_lanes(hd)] += _dot(w_ref[:, cols],
                                                        stacked_values(item_start(g), group_keys, hd))

            for hd in range(n_heads):
                z_ref[hd] = scores(qh[hd], hd, item_start(0), group_keys)
            w_ref[...] = jnp.zeros_like(w_ref)

            def key_group(state):
                g, _ = state
                pv(g - 1)
                no_mask = [None] * group
                suffixes = suffix_sums(softplus_keys(z_ref[0], no_mask), group_widths)
                for hd in range(n_heads):
                    ahead = (suffix_sums(softplus_keys(z_ref[hd + 1], no_mask), group_widths)
                             if hd + 1 < n_heads else None)
                    w_head = weights_from(z_ref[hd], suffixes, hd, qrows, group_widths, no_mask)
                    w_ref[:, hd * group_keys:(hd + 1) * group_keys] = jnp.concatenate(w_head, axis=1)
                    z_ref[hd] = scores(qh[hd], hd, item_start(g + 1), group_keys)
                    suffixes = ahead
                return g + 1, weights_live()

            n_done, live = lax.while_loop(lambda state: jnp.logical_and(state[0] < qi - 1, state[1] > 0),
                                          key_group, (jnp.int32(0), jnp.int32(1)))
            pv(n_done - 1)

            @pl.when(live > 0)
            def _():
                sweep(qh, qrows, 0, tail_widths, None)

        write(row0, qrows)
        return 0

    lax.fori_loop(1, (q_ref.shape[0] - blk) // qrows, wide_block, 0)


def _suffix_matrix():
    j = jnp.arange(ATTN_SUFFIX_BLOCK)[:, None]
    s = jnp.arange(ATTN_SUFFIX_BLOCK)[None, :]
    return -(j >= s).astype(_BF16)


def _sb_attention(q, kt, v):
    b, tp, d = q.shape
    assert (tp - ATTN_BLOCK) % ATTN_Q_ROWS == 0 and (ATTN_Q_ROWS // ATTN_BLOCK) % ATTN_KEY_GROUP == 0
    lanes = ATTN_HEADS_PER_STEP * HEAD_DIM
    return pl.pallas_call(
        _attn_kernel,
        grid=(b, d // lanes),
        in_specs=[pl.BlockSpec((None, tp, lanes), lambda i, j: (i, 0, j)),
                  pl.BlockSpec((None, lanes, tp), lambda i, j: (i, j, 0)),
                  pl.BlockSpec((None, tp, lanes), lambda i, j: (i, 0, j)),
                  _resident((ATTN_SUFFIX_BLOCK, ATTN_SUFFIX_BLOCK))],
        out_specs=pl.BlockSpec((None, tp, lanes), lambda i, j: (i, 0, j)),
        out_shape=jax.ShapeDtypeStruct((b, tp, d), _BF16),
        scratch_shapes=[pltpu.VMEM((ATTN_Q_ROWS, lanes), _F32),
                        pltpu.VMEM((ATTN_HEADS_PER_STEP, ATTN_Q_ROWS, LANES), _F32),
                        pltpu.VMEM((ATTN_HEADS_PER_STEP, ATTN_Q_ROWS, ATTN_KEY_GROUP * ATTN_BLOCK), _F32),
                        pltpu.VMEM((ATTN_Q_ROWS, ATTN_HEADS_PER_STEP * ATTN_KEY_GROUP * ATTN_BLOCK), _BF16)],
        compiler_params=pltpu.CompilerParams(dimension_semantics=("parallel", "parallel"),
                                             vmem_limit_bytes=VMEM_LIMIT),
        name="sb_attn",
    )(q, kt, v, _suffix_matrix())


def _gelu_tanh(x):
    return 0.5 * x * (1.0 + jnp.tanh(math.sqrt(2.0 / math.pi) * (x + 0.044715 * (x * x * x))))


def _sigmoid(x):
    return 1.0 / (1.0 + jnp.exp(-x))


def _lru_stage(h_ref, g_ref, win_ref, cw_ref, cb_ref, wg_ref, brg_ref, big_ref, lam_ref, y_ref,
               rec_ref, a_ref, b_ref, state_ref):
    rows, d = h_ref.shape
    tail = SUBLANES

    hn = _rms_norm(h_ref[...], g_ref[...]).astype(_BF16)
    gate = _gelu_tanh(_dot(hn, win_ref[:, :d]))
    rec_ref[tail:, :] = _dot(hn, win_ref[:, d:])
    yield

    u = cb_ref[...] + cw_ref[CONV_WIDTH - 1:CONV_WIDTH, :] * rec_ref[tail:, :]
    for j in range(CONV_WIDTH - 1):
        shift = CONV_WIDTH - 1 - j
        u = u + cw_ref[j:j + 1, :] * rec_ref[tail - shift:tail - shift + rows, :]
    rec_ref[0:tail, :] = rec_ref[rows:rows + tail, :]
    yield

    neg_lam = -lam_ref[...]
    log_a_unit = -LRU_C * (jnp.maximum(neg_lam, 0.0) + jnp.log1p(jnp.exp(-jnp.abs(neg_lam))))
    ub = u.astype(_BF16)
    for n in range(LRU_BLOCKS):
        cols = slice(n * LRU_BLOCK_DIM, (n + 1) * LRU_BLOCK_DIM)
        ri = _dot(ub[:, cols], wg_ref[n])
        r = _sigmoid(ri[:, :LRU_BLOCK_DIM] + brg_ref[:, cols])
        i = _sigmoid(ri[:, LRU_BLOCK_DIM:] + big_ref[:, cols])
        log_a = log_a_unit[:, cols] * r
        a = jnp.exp(log_a)
        one_minus_a2 = -jnp.tanh(log_a) * (1.0 + a * a)
        mult = jnp.where(one_minus_a2 > 0.0, one_minus_a2 * lax.rsqrt(one_minus_a2), 0.0)
        a_ref[:, cols] = a
        b_ref[:, cols] = mult * (i * u[:, cols])
        yield

    sub = lax.broadcasted_iota(jnp.int32, (SUBLANES, d), 0)
    state = state_ref[...]
    for gidx in range(rows // SUBLANES):
        r0 = gidx * SUBLANES
        av = a_ref[r0:r0 + SUBLANES, :]
        bv = b_ref[r0:r0 + SUBLANES, :]
        for s in (1, 2, 4):
            keep = sub >= s
            bv = jnp.where(keep, av * pltpu.roll(bv, s, axis=0) + bv, bv)
            av = jnp.where(keep, av * pltpu.roll(av, s, axis=0), av)
        hv = av * state + bv
        b_ref[r0:r0 + SUBLANES, :] = hv
        state = jnp.broadcast_to(hv[SUBLANES - 1:SUBLANES, :], (SUBLANES, d))
        if gidx % LRU_SCAN_GROUPS_PER_PIECE == LRU_SCAN_GROUPS_PER_PIECE - 1:
            yield
    state_ref[...] = state
    y_ref[...] = (b_ref[...] * gate).astype(y_ref.dtype)


def _mlp_stage(y, h, wo_ref, g_ref, wup_ref, wdn_ref, gf_ref, o_ref, hn_ref, final_norm, ff_chunk):
    d_ff = wup_ref.shape[1]
    h1 = h + _dot(y, wo_ref[...])
    hn_ref[...] = _rms_norm(h1, g_ref[...]).astype(_BF16)
    o_ref[...] = h1
    yield
    for c in range(0, d_ff, ff_chunk):
        up = jnp.maximum(_dot(hn_ref[...], wup_ref[:, c:c + ff_chunk]), 0.0)
        up = (up * up).astype(_BF16)
        yield
        o_ref[...] += _dot(up, wdn_ref[c:c + ff_chunk, :])
        yield
    if final_norm:
        o_ref[...] = _rms_norm(o_ref[...], gf_ref[...])


def _interleave(*stages):
    live = list(stages)
    while live:
        for stage in list(live):
            if next(stage, StopIteration) is StopIteration:
                live.remove(stage)


def _mlp_kernel(y_ref, h_ref, wo_ref, g_ref, wup_ref, wdn_ref, gf_ref, o_ref, hn_ref, *, final_norm):
    _interleave(_mlp_stage(y_ref[...], h_ref[...], wo_ref, g_ref, wup_ref, wdn_ref, gf_ref, o_ref, hn_ref,
                           final_norm, MLP_FF_CHUNK))


def _proj_mlp(y, h, w_o, g, w_up, w_down, g_final, final_norm):
    m, d = h.shape
    d_ff = w_up.shape[1]
    row_spec = pl.BlockSpec((MLP_ROWS, d), lambda i: (i, 0))
    return pl.pallas_call(
        functools.partial(_mlp_kernel, final_norm=final_norm),
        grid=(m // MLP_ROWS,),
        in_specs=[row_spec, row_spec, _resident((d, d)), _resident((1, d)), _resident((d, d_ff)),
                  _resident((d_ff, d)), _resident((1, d))],
        out_specs=row_spec,
        out_shape=jax.ShapeDtypeStruct((m, d), _F32),
        scratch_shapes=[pltpu.VMEM((MLP_ROWS, d), _BF16)],
        compiler_params=pltpu.CompilerParams(dimension_semantics=("parallel",), vmem_limit_bytes=VMEM_LIMIT),
        name="proj_mlp",
    )(y, h, w_o, g, w_up, w_down, g_final)


def _lru_layer_kernel(hcur_ref, hprev_ref, g_ref, win_ref, cw_ref, cb_ref, wg_ref, brg_ref, big_ref, lam_ref,
                      wo_ref, gm_ref, wup_ref, wdn_ref, gf_ref, out_ref,
                      rec_ref, a_ref, b_ref, state_ref, y_ref, hn_ref, obuf_ref, *, steps_per_batch, n_blocks):
    s = pl.program_id(0)
    rows = hcur_ref.shape[0]
    cur, prev = s % 2, 1 - s % 2

    @pl.when(s == 0)
    def _():
        y_ref[...] = jnp.zeros_like(y_ref)
        obuf_ref[...] = jnp.zeros_like(obuf_ref)

    @pl.when(s % steps_per_batch == 0)
    def _():
        rec_ref[0:SUBLANES, :] = jnp.zeros((SUBLANES, rec_ref.shape[1]), _F32)
        state_ref[...] = jnp.zeros_like(state_ref)

    @pl.when(s <= n_blocks)
    def _():
        y_prev = y_ref[...]
        _interleave(
            _lru_stage(hcur_ref, g_ref, win_ref, cw_ref, cb_ref, wg_ref, brg_ref, big_ref, lam_ref, y_ref,
                       rec_ref, a_ref, b_ref, state_ref),
            _mlp_stage(y_prev, hprev_ref[...], wo_ref, gm_ref, wup_ref, wdn_ref, gf_ref,
                       obuf_ref.at[cur], hn_ref, True, LRU_LAYER_FF_CHUNK))

    out_ref[0:rows - N_META, :] = obuf_ref[prev, N_META:rows, :]
    out_ref[rows - N_META:rows, :] = obuf_ref[cur, 0:N_META, :]


def _lru_layer(h, batch, seq, g, w_in, conv_w, conv_b, w_gates, b_rg, b_ig, lam, w_out, g_mlp, w_up, w_down,
               g_final):
    m, d = h.shape
    d_ff = w_up.shape[1]
    n_blocks = m // TIME_BLOCK
    steps_per_batch = n_blocks // batch
    cur_spec = pl.BlockSpec((TIME_BLOCK, d), lambda s: (jnp.minimum(s, n_blocks - 1), 0))
    prev_spec = pl.BlockSpec((TIME_BLOCK, d), lambda s: (jnp.clip(s - 1, 0, n_blocks - 1), 0))

    def out_index(s):
        block = jnp.maximum(s - 2, 0)
        return block // steps_per_batch, block % steps_per_batch, 0

    return pl.pallas_call(
        functools.partial(_lru_layer_kernel, steps_per_batch=steps_per_batch, n_blocks=n_blocks),
        grid=(n_blocks + 2,),
        in_specs=[cur_spec, prev_spec, _resident((1, d)), _resident((d, 2 * d)), _resident((CONV_WIDTH, d)),
                  _resident((1, d)), _resident((LRU_BLOCKS, LRU_BLOCK_DIM, 2 * LRU_BLOCK_DIM)),
                  _resident((1, d)), _resident((1, d)), _resident((1, d)),
                  _resident((d, d)), _resident((1, d)), _resident((d, d_ff)), _resident((d_ff, d)),
                  _resident((1, d))],
        out_specs=pl.BlockSpec((None, TIME_BLOCK, d), out_index),
        out_shape=jax.ShapeDtypeStruct((batch, seq, d), _F32),
        scratch_shapes=[pltpu.VMEM((SUBLANES + TIME_BLOCK, d), _F32), pltpu.VMEM((TIME_BLOCK, d), _F32),
                        pltpu.VMEM((TIME_BLOCK, d), _F32), pltpu.VMEM((SUBLANES, d), _F32),
                        pltpu.VMEM((TIME_BLOCK, d), _BF16), pltpu.VMEM((TIME_BLOCK, d), _BF16),
                        pltpu.VMEM((2, TIME_BLOCK, d), _F32)],
        compiler_params=pltpu.CompilerParams(dimension_semantics=("arbitrary",), vmem_limit_bytes=VMEM_LIMIT),
        name="lru_layer",
    )(h, h, g, w_in, conv_w, conv_b, w_gates, b_rg, b_ig, lam, w_out, g_mlp, w_up, w_down, g_final)


def kernel(x, meta_tokens, norm_mix, norm_mlp, sb_w_qkv, sb_w_o, lru_w_in, lru_conv_w, lru_conv_b, lru_w_rg,
           lru_b_rg, lru_w_ig, lru_b_ig, lru_lambda, lru_w_out, mlp_w_up, mlp_w_down, norm_final):
    b, seq, d = x.shape
    assert d == HEADS * HEAD_DIM == LRU_BLOCKS * LRU_BLOCK_DIM
    t_len = N_META + seq
    tp = -(-t_len // TIME_BLOCK) * TIME_BLOCK
    assert (b * tp) % MLP_ROWS == 0 and tp % ATTN_BLOCK == 0

    row = lambda v: v.reshape(1, d)
    w_qkv = sb_w_qkv[0]
    q, kt, v, h = _qkv(x, meta_tokens.astype(x.dtype), tp, row(norm_mix[0]), w_qkv[:, :d].astype(_BF16),
                       w_qkv[:, d:2 * d].T.astype(_BF16), w_qkv[:, 2 * d:].astype(_BF16))
    o = _sb_attention(q, kt, v)
    h = _proj_mlp(o.reshape(b * tp, d), h.reshape(b * tp, d), sb_w_o[0].astype(_BF16), row(norm_mlp[0]),
                  mlp_w_up[0].astype(_BF16), mlp_w_down[0].astype(_BF16), row(norm_final), False)

    w_gates = jnp.concatenate([lru_w_rg[0], lru_w_ig[0]], axis=-1).astype(_BF16)
    return _lru_layer(h, b, seq, row(norm_mix[1]), lru_w_in[0].astype(_BF16), lru_conv_w[0],
                      row(lru_conv_b[0]), w_gates, row(lru_b_rg[0]), row(lru_b_ig[0]), row(lru_lambda[0]),
                      lru_w_out[0].astype(_BF16), row(norm_mlp[1]), mlp_w_up[1].astype(_BF16),
                      mlp_w_down[1].astype(_BF16), row(norm_final))
```

```python
import functools
import math

import jax
import jax.numpy as jnp
from jax import lax
from jax.experimental import pallas as pl
from jax.experimental.pallas import tpu as pltpu

N_META = 16
HEADS = 16
HEAD_DIM = 64
LRU_BLOCKS = 8
LRU_BLOCK_DIM = 128
CONV_WIDTH = 4
LRU_C = 8.0
EPS = 1e-6
LOG2E = 1.4426950408889634

LANES = 128
SUBLANES = 8
TIME_BLOCK = 384
ATTN_BLOCK = 128
ATTN_Q_ROWS = 256
ATTN_KEY_GROUP = 2
ATTN_SUFFIX_BLOCK = 256
ATTN_HEAD_GROUP = 4
ATTN_HEADS_PER_STEP = 8
ATTN_LOG2_WEIGHT_FLOOR = -192.0
MLP_ROWS = 512
MLP_FF_CHUNK = 1024
LRU_SCAN_GROUPS_PER_PIECE = 8
LRU_LAYER_FF_CHUNK = 512
VMEM_LIMIT = 56 * 1024 * 1024

_BF16 = jnp.bfloat16
_F32 = jnp.float32


def _dot(a, b):
    return jnp.dot(a, b, preferred_element_type=_F32)


def _rms_norm(x, g):
    ms = jnp.mean(x * x, axis=-1, keepdims=True)
    return (x * lax.rsqrt(ms + EPS)) * g


def _resident(shape):
    zeros = (0,) * len(shape)
    return pl.BlockSpec(shape, lambda *_: zeros, pipeline_mode=pl.Buffered(1))


def _qkv_kernel(x_ref, meta_ref, g_ref, w_ref, q_ref, kt_ref, v_ref, h_ref, *, pad_rows):
    j, last = pl.program_id(1), pl.num_programs(1) - 1
    rows, d = x_ref.shape
    x = x_ref[...]
    first_block = jnp.concatenate([meta_ref[...], x[:rows - N_META]], axis=0)
    last_block = jnp.concatenate([x[pad_rows:], jnp.zeros((pad_rows, d), x.dtype)], axis=0)
    h = jnp.where(j == 0, first_block, jnp.where(j == last, last_block, x))
    h_ref[...] = h
    hn = _rms_norm(h, g_ref[...]).astype(_BF16)
    q_ref[...] = (_dot(hn, w_ref[:, :d].astype(_BF16)) * (HEAD_DIM ** -0.5 * LOG2E)).astype(_BF16)
    v_ref[...] = _dot(hn, w_ref[:, 2 * d:].astype(_BF16)).astype(_BF16)
    kt = lax.dot_general(w_ref[:, d:2 * d].astype(_BF16), hn, (((0,), (1,)), ((), ())),
                         preferred_element_type=_F32)
    kt_ref[...] = kt.astype(_BF16)


def _qkv(x, meta, tp, g, w_qkv):
    b, seq, d = x.shape
    nt = tp // TIME_BLOCK
    pad_rows = tp - N_META - seq
    assert nt >= 2 and seq >= TIME_BLOCK and all(n % SUBLANES == 0 for n in (pad_rows, N_META, seq, TIME_BLOCK))
    row_spec = pl.BlockSpec((None, TIME_BLOCK, d), lambda i, j: (i, j, 0))
    tiles, meta_tiles = TIME_BLOCK // SUBLANES, N_META // SUBLANES
    x_spec = pl.BlockSpec(
        (pl.Squeezed(), pl.Element(TIME_BLOCK), pl.Element(d)),
        lambda i, j: (i, jnp.clip(j * tiles - meta_tiles, 0, (seq - TIME_BLOCK) // SUBLANES) * SUBLANES, 0))
    return pl.pallas_call(
        functools.partial(_qkv_kernel, pad_rows=pad_rows),
        grid=(b, nt),
        in_specs=[x_spec, _resident((N_META, d)), _resident((1, d)), _resident((d, 3 * d))],
        out_specs=[row_spec, pl.BlockSpec((None, d, TIME_BLOCK), lambda i, j: (i, 0, j)), row_spec, row_spec],
        out_shape=[jax.ShapeDtypeStruct((b, tp, d), _BF16), jax.ShapeDtypeStruct((b, d, tp), _BF16),
                   jax.ShapeDtypeStruct((b, tp, d), _BF16), jax.ShapeDtypeStruct((b, tp, d), _F32)],
        compiler_params=pltpu.CompilerParams(dimension_semantics=("parallel", "parallel"),
                                             vmem_limit_bytes=VMEM_LIMIT),
        name="qkv",
    )(x, meta, g, w_qkv)


def _attn_kernel(q_ref, kt_ref, v_ref, nu_ref, o_ref, acc_ref, carry_ref, z_ref, w_ref):
    blk, qrows, group = ATTN_BLOCK, ATTN_Q_ROWS, ATTN_KEY_GROUP
    n_heads = q_ref.shape[1] // HEAD_DIM
    lanes = ATTN_HEAD_GROUP * HEAD_DIM

    def group_lanes(hd):
        lo = hd // ATTN_HEAD_GROUP * lanes
        return slice(lo, lo + lanes)

    def head_of_lane(rows):
        return lax.broadcasted_iota(jnp.int32, (rows, lanes), 1) // HEAD_DIM

    def stacked_values(key_blk, keys, first_head):
        v = v_ref[pl.ds(pl.multiple_of(key_blk * blk, blk), keys), group_lanes(first_head)]
        v_head = head_of_lane(keys)
        return jnp.concatenate([jnp.where(v_head == hd, v, jnp.zeros_like(v)) for hd in range(ATTN_HEAD_GROUP)],
                               axis=0)

    def scores(q_head, hd, key_blk, keys):
        return _dot(q_head, kt_ref[group_lanes(hd), pl.ds(pl.multiple_of(key_blk * blk, blk), keys)])

    def visibility(rows, n_blk, causal_shift):
        masks = []
        for k_idx in range(n_blk):
            if causal_shift is None or (k_idx + 1) * blk - causal_shift <= 0:
                masks.append(None)
            else:
                key_pos = lax.broadcasted_iota(jnp.int32, (rows, blk), 1) + (k_idx * blk - causal_shift)
                masks.append(key_pos < lax.broadcasted_iota(jnp.int32, (rows, blk), 0))
        return masks

    def softplus_keys(z, masks):
        sp = jnp.maximum(jnp.log2(1.0 + jnp.exp2(jnp.minimum(z, 100.0))), z)
        return jnp.concatenate([sp[:, k * blk:(k + 1) * blk] if m is None else
                                jnp.where(m, sp[:, k * blk:(k + 1) * blk], 0.0)
                                for k, m in enumerate(masks)], axis=1).astype(_BF16)

    def suffix_sums(sp, widths):
        out, lo = [], 0
        for width in widths:
            out.append(_dot(sp[:, lo:lo + width], nu_ref[:width, :width]))
            lo += width
        return out

    def weights_from(z, suffixes, hd, rows, widths, masks):
        carry = carry_ref[hd, :rows, :]
        w_head = [None] * len(masks)
        lo = sum(widths)
        for width, suffix in zip(reversed(widths), reversed(suffixes)):
            lo -= width
            for c in range(width // blk):
                k_idx = lo // blk + c
                w = jnp.exp2(z[:, k_idx * blk:(k_idx + 1) * blk] + suffix[:, c * blk:(c + 1) * blk] + carry)
                if masks[k_idx] is not None:
                    w = jnp.where(masks[k_idx], w, 0.0)
                w_head[k_idx] = w.astype(_BF16)
            carry = carry + jnp.broadcast_to(suffix[:, 0:1], (rows, LANES))
        carry_ref[hd, :rows, :] = carry
        return w_head

    def sweep(qh, rows, key_blk, widths, causal_shift):
        keys = sum(widths)
        masks = visibility(rows, keys // blk, causal_shift)
        v = v_ref[pl.ds(pl.multiple_of(key_blk * blk, blk), keys), :]
        v_head = head_of_lane(keys)
        zs, suffixes = {}, {}
        for tick in range(n_heads + 2):
            if tick < n_heads:
                zs[tick] = scores(qh[tick], tick, key_blk, keys)
            if 0 <= tick - 1 < n_heads:
                suffixes[tick - 1] = suffix_sums(softplus_keys(zs[tick - 1], masks), widths)
            if 0 <= tick - 2 < n_heads:
                hd = tick - 2
                w = jnp.concatenate(weights_from(zs.pop(hd), suffixes.pop(hd), hd, rows, widths, masks), axis=1)
                v_group = v[:, group_lanes(hd)]
                acc_ref[:rows, group_lanes(hd)] += _dot(
                    w, jnp.where(v_head == hd % ATTN_HEAD_GROUP, v_group, jnp.zeros_like(v_group)))

    def start_block(row0, rows):
        q = q_ref[pl.ds(row0, rows), :]
        q_head = head_of_lane(rows)
        acc_ref[...] = jnp.zeros_like(acc_ref)
        carry_ref[...] = jnp.zeros_like(carry_ref)
        groups = [q[:, group_lanes(hd)] for hd in range(0, n_heads, ATTN_HEAD_GROUP)]
        return [jnp.where(q_head == hd % ATTN_HEAD_GROUP, groups[hd // ATTN_HEAD_GROUP],
                          jnp.zeros_like(groups[0])) for hd in range(n_heads)]

    def write(row0, rows):
        o_ref[pl.ds(row0, rows), :] = acc_ref[:rows, :].astype(o_ref.dtype)

    def weights_live():
        return (jnp.max(carry_ref[...]) > ATTN_LOG2_WEIGHT_FLOOR).astype(jnp.int32)

    wide = ATTN_SUFFIX_BLOCK
    group_widths = [wide] * (group * blk // wide)
    group_keys = group * blk
    near = wide // blk
    tail_blocks = 1 + group - near
    tail_widths = [blk] * (tail_blocks % near) + [wide] * (tail_blocks // near)
    assert group_keys == qrows and 0 < near <= group

    sweep(start_block(0, blk), blk, 0, [blk], 0)
    write(0, blk)
    sweep(start_block(blk, qrows), qrows, 0, [blk] + group_widths, blk)
    write(blk, qrows)

    def wide_block(qi, _):
        row0 = pl.multiple_of(blk + qi * qrows, blk)
        first = 1 + qi * group
        qh = start_block(row0, qrows)
        sweep(qh, qrows, first - near, [wide] + group_widths, wide)

        @pl.when(weights_live() > 0)
        def _():
            def item_start(g):
                return jnp.maximum(first - near - (g + 1) * group, 0)

            def pv(g):
                for hd in range(0, n_heads, ATTN_HEAD_GROUP):
                    cols = slice(hd * group_keys, (hd + ATTN_HEAD_GROUP) * group_keys)
                    acc_ref[:, group_lanes(hd)] += _dot(w_ref[:, cols],
                                                        stacked_values(item_start(g), group_keys, hd))

            for hd in range(n_heads):
                z_ref[hd] = scores(qh[hd], hd, item_start(0), group_keys)
            w_ref[...] = jnp.zeros_like(w_ref)

            def key_group(state):
                g, _ = state
                pv(g - 1)
                no_mask = [None] * group
                suffixes = suffix_sums(softplus_keys(z_ref[0], no_mask), group_widths)
                for hd in range(n_heads):
                    ahead = (suffix_sums(softplus_keys(z_ref[hd + 1], no_mask), group_widths)
                             if hd + 1 < n_heads else None)
                    w_head = weights_from(z_ref[hd], suffixes, hd, qrows, group_widths, no_mask)
                    w_ref[:, hd * group_keys:(hd + 1) * group_keys] = jnp.concatenate(w_head, axis=1)
                    z_ref[hd] = scores(qh[hd], hd, item_start(g + 1), group_keys)
                    suffixes = ahead
                return g + 1, weights_live()

            n_done, live = lax.while_loop(lambda state: jnp.logical_and(state[0] < qi - 1, state[1] > 0),
                                          key_group, (jnp.int32(0), jnp.int32(1)))
            pv(n_done - 1)

            @pl.when(live > 0)
            def _():
                sweep(qh, qrows, 0, tail_widths, None)

        write(row0, qrows)
        return 0

    lax.fori_loop(1, (q_ref.shape[0] - blk) // qrows, wide_block, 0)


def _suffix_matrix():
    j = jnp.arange(ATTN_SUFFIX_BLOCK)[:, None]
    s = jnp.arange(ATTN_SUFFIX_BLOCK)[None, :]
    return -(j >= s).astype(_BF16)


def _sb_attention(q, kt, v):
    b, tp, d = q.shape
    assert (tp - ATTN_BLOCK) % ATTN_Q_ROWS == 0 and (ATTN_Q_ROWS // ATTN_BLOCK) % ATTN_KEY_GROUP == 0
    lanes = ATTN_HEADS_PER_STEP * HEAD_DIM
    return pl.pallas_call(
        _attn_kernel,
        grid=(b, d // lanes),
        in_specs=[pl.BlockSpec((None, tp, lanes), lambda i, j: (i, 0, j)),
                  pl.BlockSpec((None, lanes, tp), lambda i, j: (i, j, 0)),
                  pl.BlockSpec((None, tp, lanes), lambda i, j: (i, 0, j)),
                  _resident((ATTN_SUFFIX_BLOCK, ATTN_SUFFIX_BLOCK))],
        out_specs=pl.BlockSpec((None, tp, lanes), lambda i, j: (i, 0, j)),
        out_shape=jax.ShapeDtypeStruct((b, tp, d), _BF16),
        scratch_shapes=[pltpu.VMEM((ATTN_Q_ROWS, lanes), _F32),
                        pltpu.VMEM((ATTN_HEADS_PER_STEP, ATTN_Q_ROWS, LANES), _F32),
                        pltpu.VMEM((ATTN_HEADS_PER_STEP, ATTN_Q_ROWS, ATTN_KEY_GROUP * ATTN_BLOCK), _F32),
                        pltpu.VMEM((ATTN_Q_ROWS, ATTN_HEADS_PER_STEP * ATTN_KEY_GROUP * ATTN_BLOCK), _BF16)],
        compiler_params=pltpu.CompilerParams(dimension_semantics=("parallel", "parallel"),
                                             vmem_limit_bytes=VMEM_LIMIT),
        name="sb_attn",
    )(q, kt, v, _suffix_matrix())


def _gelu_tanh(x):
    c = math.sqrt(2.0 / math.pi)
    half = 0.5 * x
    return half + half * jnp.tanh(x * ((0.044715 * c) * (x * x) + c))


def _sigmoid(x):
    return 0.5 + 0.5 * jnp.tanh(0.5 * x)


def _lru_stage(h_ref, g_ref, win_ref, cw_ref, cb_ref, wg_ref, brg_ref, big_ref, lam_ref, y_ref,
               rec_ref, a_ref, b_ref, state_ref):
    rows, d = h_ref.shape
    tail = SUBLANES

    hn = _rms_norm(h_ref[...], g_ref[...]).astype(_BF16)
    gate = _gelu_tanh(_dot(hn, win_ref[:, :d]))
    rec_ref[tail:, :] = _dot(hn, win_ref[:, d:])
    yield

    u = cb_ref[...] + cw_ref[CONV_WIDTH - 1:CONV_WIDTH, :] * rec_ref[tail:, :]
    for j in range(CONV_WIDTH - 1):
        shift = CONV_WIDTH - 1 - j
        u = u + cw_ref[j:j + 1, :] * rec_ref[tail - shift:tail - shift + rows, :]
    rec_ref[0:tail, :] = rec_ref[rows:rows + tail, :]
    yield

    neg_lam = -lam_ref[...]
    log_a_unit = -LRU_C * (jnp.maximum(neg_lam, 0.0) + jnp.log1p(jnp.exp(-jnp.abs(neg_lam))))
    ub = u.astype(_BF16)
    for n in range(LRU_BLOCKS):
        cols = slice(n * LRU_BLOCK_DIM, (n + 1) * LRU_BLOCK_DIM)
        ri = _dot(ub[:, cols], wg_ref[n])
        r = _sigmoid(ri[:, :LRU_BLOCK_DIM] + brg_ref[:, cols])
        i = _sigmoid(ri[:, LRU_BLOCK_DIM:] + big_ref[:, cols])
        log_a = log_a_unit[:, cols] * r
        a = jnp.exp(log_a)
        one_minus_a2 = -jnp.tanh(log_a) * (1.0 + a * a)
        mult = jnp.where(one_minus_a2 > 0.0, one_minus_a2 * lax.rsqrt(one_minus_a2), 0.0)
        a_ref[:, cols] = a
        b_ref[:, cols] = mult * (i * u[:, cols])
        yield

    sub = lax.broadcasted_iota(jnp.int32, (SUBLANES, d), 0)
    state = state_ref[...]
    for gidx in range(rows // SUBLANES):
        r0 = gidx * SUBLANES
        av = a_ref[r0:r0 + SUBLANES, :]
        bv = b_ref[r0:r0 + SUBLANES, :]
        for s in (1, 2, 4):
            keep = sub >= s
            bv = jnp.where(keep, av * pltpu.roll(bv, s, axis=0) + bv, bv)
            av = jnp.where(keep, av * pltpu.roll(av, s, axis=0), av)
        hv = av * state + bv
        b_ref[r0:r0 + SUBLANES, :] = hv
        state = jnp.broadcast_to(hv[SUBLANES - 1:SUBLANES, :], (SUBLANES, d))
        if gidx % LRU_SCAN_GROUPS_PER_PIECE == LRU_SCAN_GROUPS_PER_PIECE - 1:
            yield
    state_ref[...] = state
    y_ref[...] = (b_ref[...] * gate).astype(y_ref.dtype)


def _mlp_stage(y, h, wo_ref, g_ref, wup_ref, wdn_ref, gf_ref, o_ref, hn_ref, final_norm, ff_chunk):
    d_ff = wup_ref.shape[1]
    h1 = h + _dot(y, wo_ref[...])
    hn_ref[...] = _rms_norm(h1, g_ref[...]).astype(_BF16)
    o_ref[...] = h1
    yield
    for c in range(0, d_ff, ff_chunk):
        up = jnp.maximum(_dot(hn_ref[...], wup_ref[:, c:c + ff_chunk]), 0.0)
        up = (up * up).astype(_BF16)
        yield
        o_ref[...] += _dot(up, wdn_ref[c:c + ff_chunk, :])
        yield
    if final_norm:
        o_ref[...] = _rms_norm(o_ref[...], gf_ref[...])


def _interleave(*stages):
    live = list(stages)
    while live:
        for stage in list(live):
            if next(stage, StopIteration) is StopIteration:
                live.remove(stage)


def _mlp_kernel(y_ref, h_ref, wo_ref, g_ref, wup_ref, wdn_ref, gf_ref, o_ref, hn_ref, *, final_norm):
    _interleave(_mlp_stage(y_ref[...], h_ref[...], wo_ref, g_ref, wup_ref, wdn_ref, gf_ref, o_ref, hn_ref,
                           final_norm, MLP_FF_CHUNK))


def _proj_mlp(y, h, w_o, g, w_up, w_down, g_final, final_norm):
    m, d = h.shape
    d_ff = w_up.shape[1]
    row_spec = pl.BlockSpec((MLP_ROWS, d), lambda i: (i, 0))
    return pl.pallas_call(
        functools.partial(_mlp_kernel, final_norm=final_norm),
        grid=(m // MLP_ROWS,),
        in_specs=[row_spec, row_spec, _resident((d, d)), _resident((1, d)), _resident((d, d_ff)),
                  _resident((d_ff, d)), _resident((1, d))],
        out_specs=row_spec,
        out_shape=jax.ShapeDtypeStruct((m, d), _F32),
        scratch_shapes=[pltpu.VMEM((MLP_ROWS, d), _BF16)],
        compiler_params=pltpu.CompilerParams(dimension_semantics=("parallel",), vmem_limit_bytes=VMEM_LIMIT),
        name="proj_mlp",
    )(y, h, w_o, g, w_up, w_down, g_final)


def _lru_layer_kernel(hcur_ref, hprev_ref, g_ref, win_ref, cw_ref, cb_ref, wg_ref, brg_ref, big_ref, lam_ref,
                      wo_ref, gm_ref, wup_ref, wdn_ref, gf_ref, out_ref,
                      rec_ref, a_ref, b_ref, state_ref, y_ref, hn_ref, obuf_ref, *, steps_per_batch, n_blocks):
    s = pl.program_id(0)
    rows = hcur_ref.shape[0]
    cur, prev = s % 2, 1 - s % 2

    @pl.when(s == 0)
    def _():
        y_ref[...] = jnp.zeros_like(y_ref)
        obuf_ref[...] = jnp.zeros_like(obuf_ref)

    @pl.when(s % steps_per_batch == 0)
    def _():
        rec_ref[0:SUBLANES, :] = jnp.zeros((SUBLANES, rec_ref.shape[1]), _F32)
        state_ref[...] = jnp.zeros_like(state_ref)

    @pl.when(s <= n_blocks)
    def _():
        y_prev = y_ref[...]
        _interleave(
            _lru_stage(hcur_ref, g_ref, win_ref, cw_ref, cb_ref, wg_ref, brg_ref, big_ref, lam_ref, y_ref,
                       rec_ref, a_ref, b_ref, state_ref),
            _mlp_stage(y_prev, hprev_ref[...], wo_ref, gm_ref, wup_ref, wdn_ref, gf_ref,
                       obuf_ref.at[cur], hn_ref, True, LRU_LAYER_FF_CHUNK))

    out_ref[0:rows - N_META, :] = obuf_ref[prev, N_META:rows, :]
    out_ref[rows - N_META:rows, :] = obuf_ref[cur, 0:N_META, :]


def _lru_layer(h, batch, seq, g, w_in, conv_w, conv_b, w_gates, b_rg, b_ig, lam, w_out, g_mlp, w_up, w_down,
               g_final):
    m, d = h.shape
    d_ff = w_up.shape[1]
    n_blocks = m // TIME_BLOCK
    steps_per_batch = n_blocks // batch
    cur_spec = pl.BlockSpec((TIME_BLOCK, d), lambda s: (jnp.minimum(s, n_blocks - 1), 0))
    prev_spec = pl.BlockSpec((TIME_BLOCK, d), lambda s: (jnp.clip(s - 1, 0, n_blocks - 1), 0))

    def out_index(s):
        block = jnp.maximum(s - 2, 0)
        return block // steps_per_batch, block % steps_per_batch, 0

    return pl.pallas_call(
        functools.partial(_lru_layer_kernel, steps_per_batch=steps_per_batch, n_blocks=n_blocks),
        grid=(n_blocks + 2,),
        in_specs=[cur_spec, prev_spec, _resident((1, d)), _resident((d, 2 * d)), _resident((CONV_WIDTH, d)),
                  _resident((1, d)), _resident((LRU_BLOCKS, LRU_BLOCK_DIM, 2 * LRU_BLOCK_DIM)),
                  _resident((1, d)), _resident((1, d)), _resident((1, d)),
                  _resident((d, d)), _resident((1, d)), _resident((d, d_ff)), _resident((d_ff, d)),
                  _resident((1, d))],
        out_specs=pl.BlockSpec((None, TIME_BLOCK, d), out_index),
        out_shape=jax.ShapeDtypeStruct((batch, seq, d), _F32),
        scratch_shapes=[pltpu.VMEM((SUBLANES + TIME_BLOCK, d), _F32), pltpu.VMEM((TIME_BLOCK, d), _F32),
                        pltpu.VMEM((TIME_BLOCK, d), _F32), pltpu.VMEM((SUBLANES, d), _F32),
                        pltpu.VMEM((TIME_BLOCK, d), _BF16), pltpu.VMEM((TIME_BLOCK, d), _BF16),
                        pltpu.VMEM((2, TIME_BLOCK, d), _F32)],
        compiler_params=pltpu.CompilerParams(dimension_semantics=("arbitrary",), vmem_limit_bytes=VMEM_LIMIT),
        name="lru_layer",
    )(h, h, g, w_in, conv_w, conv_b, w_gates, b_rg, b_ig, lam, w_out, g_mlp, w_up, w_down, g_final)


def kernel(x, meta_tokens, norm_mix, norm_mlp, sb_w_qkv, sb_w_o, lru_w_in, lru_conv_w, lru_conv_b, lru_w_rg,
           lru_b_rg, lru_w_ig, lru_b_ig, lru_lambda, lru_w_out, mlp_w_up, mlp_w_down, norm_final):
    b, seq, d = x.shape
    assert d == HEADS * HEAD_DIM == LRU_BLOCKS * LRU_BLOCK_DIM
    t_len = N_META + seq
    tp = -(-t_len // TIME_BLOCK) * TIME_BLOCK
    assert (b * tp) % MLP_ROWS == 0 and tp % ATTN_BLOCK == 0

    row = lambda v: v.reshape(1, d)
    q, kt, v, h = _qkv(x, meta_tokens.astype(x.dtype), tp, row(norm_mix[0]), sb_w_qkv[0])
    o = _sb_attention(q, kt, v)
    h = _proj_mlp(o.reshape(b * tp, d), h.reshape(b * tp, d), sb_w_o[0].astype(_BF16), row(norm_mlp[0]),
                  mlp_w_up[0].astype(_BF16), mlp_w_down[0].astype(_BF16), row(norm_final), False)

    w_gates = jnp.concatenate([lru_w_rg[0], lru_w_ig[0]], axis=-1).astype(_BF16)
    return _lru_layer(h, b, seq, row(norm_mix[1]), lru_w_in[0].astype(_BF16), lru_conv_w[0],
                      row(lru_conv_b[0]), w_gates, row(lru_b_rg[0]), row(lru_b_ig[0]), row(lru_lambda[0]),
                      lru_w_out[0].astype(_BF16), row(norm_mlp[1]), mlp_w_up[1].astype(_BF16),
                      mlp_w_down[1].astype(_BF16), row(norm_final))
```

```python
import functools
import math

import jax
import jax.numpy as jnp
from jax import lax
from jax.experimental import pallas as pl
from jax.experimental.pallas import tpu as pltpu

N_META = 16
HEADS = 16
HEAD_DIM = 64
LRU_BLOCKS = 8
LRU_BLOCK_DIM = 128
CONV_WIDTH = 4
LRU_C = 8.0
EPS = 1e-6
LOG2E = 1.4426950408889634

LANES = 128
SUBLANES = 8
TIME_BLOCK = 384
ATTN_BLOCK = 128
ATTN_Q_ROWS = 256
ATTN_KEY_GROUP = 2
ATTN_SUFFIX_BLOCK = 256
ATTN_HEAD_GROUP = 4
ATTN_HEADS_PER_STEP = 8
ATTN_LOG2_WEIGHT_FLOOR = -192.0
MLP_ROWS = 512
MLP_FF_CHUNK = 1024
LRU_SCAN_GROUPS_PER_PIECE = 8
LRU_LAYER_FF_CHUNK = 512
VMEM_LIMIT = 56 * 1024 * 1024

_BF16 = jnp.bfloat16
_F32 = jnp.float32


def _dot(a, b):
    return jnp.dot(a, b, preferred_element_type=_F32)


def _rms_norm(x, g):
    ms = jnp.mean(x * x, axis=-1, keepdims=True)
    return (x * lax.rsqrt(ms + EPS)) * g


def _resident(shape):
    zeros = (0,) * len(shape)
    return pl.BlockSpec(shape, lambda *_: zeros, pipeline_mode=pl.Buffered(1))


def _qkv_kernel(x_ref, meta_ref, g_ref, w_ref, q_ref, kt_ref, v_ref, h_ref, *, pad_rows):
    j, last = pl.program_id(1), pl.num_programs(1) - 1
    rows, d = x_ref.shape
    x = x_ref[...]
    first_block = jnp.concatenate([meta_ref[...], x[:rows - N_META]], axis=0)
    last_block = jnp.concatenate([x[pad_rows:], jnp.zeros((pad_rows, d), x.dtype)], axis=0)
    h = jnp.where(j == 0, first_block, jnp.where(j == last, last_block, x))
    h_ref[...] = h
    hn = _rms_norm(h, g_ref[...]).astype(_BF16)
    q_ref[...] = (_dot(hn, w_ref[:, :d].astype(_BF16)) * (HEAD_DIM ** -0.5 * LOG2E)).astype(_BF16)
    v_ref[...] = _dot(hn, w_ref[:, 2 * d:].astype(_BF16)).astype(_BF16)
    kt = lax.dot_general(w_ref[:, d:2 * d].astype(_BF16), hn, (((0,), (1,)), ((), ())),
                         preferred_element_type=_F32)
    kt_ref[...] = kt.astype(_BF16)


def _qkv(x, meta, tp, g, w_qkv):
    b, seq, d = x.shape
    nt = tp // TIME_BLOCK
    pad_rows = tp - N_META - seq
    assert nt >= 2 and seq >= TIME_BLOCK and all(n % SUBLANES == 0 for n in (pad_rows, N_META, seq, TIME_BLOCK))
    row_spec = pl.BlockSpec((None, TIME_BLOCK, d), lambda i, j: (i, j, 0))
    tiles, meta_tiles = TIME_BLOCK // SUBLANES, N_META // SUBLANES
    x_spec = pl.BlockSpec(
        (pl.Squeezed(), pl.Element(TIME_BLOCK), pl.Element(d)),
        lambda i, j: (i, jnp.clip(j * tiles - meta_tiles, 0, (seq - TIME_BLOCK) // SUBLANES) * SUBLANES, 0))
    return pl.pallas_call(
        functools.partial(_qkv_kernel, pad_rows=pad_rows),
        grid=(b, nt),
        in_specs=[x_spec, _resident((N_META, d)), _resident((1, d)), _resident((d, 3 * d))],
        out_specs=[row_spec, pl.BlockSpec((None, d, TIME_BLOCK), lambda i, j: (i, 0, j)), row_spec, row_spec],
        out_shape=[jax.ShapeDtypeStruct((b, tp, d), _BF16), jax.ShapeDtypeStruct((b, d, tp), _BF16),
                   jax.ShapeDtypeStruct((b, tp, d), _BF16), jax.ShapeDtypeStruct((b, tp, d), _F32)],
        compiler_params=pltpu.CompilerParams(dimension_semantics=("parallel", "parallel"),
                                             vmem_limit_bytes=VMEM_LIMIT),
        name="qkv",
    )(x, meta, g, w_qkv)


def _attn_kernel(q_ref, kt_ref, v_ref, nu_ref, o_ref, acc_ref, carry_ref, z_ref, w_ref):
    blk, qrows, group = ATTN_BLOCK, ATTN_Q_ROWS, ATTN_KEY_GROUP
    n_heads = q_ref.shape[1] // HEAD_DIM
    lanes = ATTN_HEAD_GROUP * HEAD_DIM

    def group_lanes(hd):
        lo = hd // ATTN_HEAD_GROUP * lanes
        return slice(lo, lo + lanes)

    def head_of_lane(rows):
        return lax.broadcasted_iota(jnp.int32, (rows, lanes), 1) // HEAD_DIM

    def stacked_values(key_blk, keys, first_head):
        v = v_ref[pl.ds(pl.multiple_of(key_blk * blk, blk), keys), group_lanes(first_head)]
        v_head = head_of_lane(keys)
        return jnp.concatenate([jnp.where(v_head == hd, v, jnp.zeros_like(v)) for hd in range(ATTN_HEAD_GROUP)],
                               axis=0)

    def scores(q_head, hd, key_blk, keys):
        return _dot(q_head, kt_ref[group_lanes(hd), pl.ds(pl.multiple_of(key_blk * blk, blk), keys)])

    def visibility(rows, n_blk, causal_shift):
        masks = []
        for k_idx in range(n_blk):
            if causal_shift is None or (k_idx + 1) * blk - causal_shift <= 0:
                masks.append(None)
            else:
                key_pos = lax.broadcasted_iota(jnp.int32, (rows, blk), 1) + (k_idx * blk - causal_shift)
                masks.append(key_pos < lax.broadcasted_iota(jnp.int32, (rows, blk), 0))
        return masks

    def softplus_keys(z, masks):
        sp = jnp.maximum(jnp.log2(1.0 + jnp.exp2(jnp.minimum(z, 100.0))), z)
        return jnp.concatenate([sp[:, k * blk:(k + 1) * blk] if m is None else
                                jnp.where(m, sp[:, k * blk:(k + 1) * blk], 0.0)
                                for k, m in enumerate(masks)], axis=1).astype(_BF16)

    def suffix_sums(sp, widths):
        out, lo = [], 0
        for width in widths:
            out.append(_dot(sp[:, lo:lo + width], nu_ref[:width, :width]))
            lo += width
        return out

    def weights_from(z, suffixes, hd, rows, widths, masks):
        carry = carry_ref[hd, :rows, :]
        w_head = [None] * len(masks)
        lo = sum(widths)
        for width, suffix in zip(reversed(widths), reversed(suffixes)):
            lo -= width
            for c in range(width // blk):
                k_idx = lo // blk + c
                w = jnp.exp2(z[:, k_idx * blk:(k_idx + 1) * blk] + suffix[:, c * blk:(c + 1) * blk] + carry)
                if masks[k_idx] is not None:
                    w = jnp.where(masks[k_idx], w, 0.0)
                w_head[k_idx] = w.astype(_BF16)
            carry = carry + jnp.broadcast_to(suffix[:, 0:1], (rows, LANES))
        carry_ref[hd, :rows, :] = carry
        return w_head

    def sweep(qh, rows, key_blk, widths, causal_shift):
        keys = sum(widths)
        masks = visibility(rows, keys // blk, causal_shift)
        v = v_ref[pl.ds(pl.multiple_of(key_blk * blk, blk), keys), :]
        v_head = head_of_lane(keys)
        zs, suffixes = {}, {}
        for tick in range(n_heads + 2):
            if tick < n_heads:
                zs[tick] = scores(qh[tick], tick, key_blk, keys)
            if 0 <= tick - 1 < n_heads:
                suffixes[tick - 1] = suffix_sums(softplus_keys(zs[tick - 1], masks), widths)
            if 0 <= tick - 2 < n_heads:
                hd = tick - 2
                w = jnp.concatenate(weights_from(zs.pop(hd), suffixes.pop(hd), hd, rows, widths, masks), axis=1)
                v_group = v[:, group_lanes(hd)]
                acc_ref[:rows, group_lanes(hd)] += _dot(
                    w, jnp.where(v_head == hd % ATTN_HEAD_GROUP, v_group, jnp.zeros_like(v_group)))

    def start_block(row0, rows):
        q = q_ref[pl.ds(row0, rows), :]
        q_head = head_of_lane(rows)
        acc_ref[...] = jnp.zeros_like(acc_ref)
        carry_ref[...] = jnp.zeros_like(carry_ref)
        groups = [q[:, group_lanes(hd)] for hd in range(0, n_heads, ATTN_HEAD_GROUP)]
        return [jnp.where(q_head == hd % ATTN_HEAD_GROUP, groups[hd // ATTN_HEAD_GROUP],
                          jnp.zeros_like(groups[0])) for hd in range(n_heads)]

    def write(row0, rows):
        o_ref[pl.ds(row0, rows), :] = acc_ref[:rows, :].astype(o_ref.dtype)

    def weights_live():
        return (jnp.max(carry_ref[...]) > ATTN_LOG2_WEIGHT_FLOOR).astype(jnp.int32)

    wide = ATTN_SUFFIX_BLOCK
    group_widths = [wide] * (group * blk // wide)
    group_keys = group * blk
    near = wide // blk
    tail_blocks = 1 + group - near
    tail_widths = [blk] * (tail_blocks % near) + [wide] * (tail_blocks // near)
    assert group_keys == qrows and 0 < near <= group

    sweep(start_block(0, blk), blk, 0, [blk], 0)
    write(0, blk)
    sweep(start_block(blk, qrows), qrows, 0, [blk] + group_widths, blk)
    write(blk, qrows)

    def wide_block(qi, _):
        row0 = pl.multiple_of(blk + qi * qrows, blk)
        first = 1 + qi * group
        qh = start_block(row0, qrows)
        sweep(qh, qrows, first - near, [wide] + group_widths, wide)

        @pl.when(weights_live() > 0)
        def _():
            def item_start(g):
                return jnp.maximum(first - near - (g + 1) * group, 0)

            def pv(g):
                for hd in range(0, n_heads, ATTN_HEAD_GROUP):
                    cols = slice(hd * group_keys, (hd + ATTN_HEAD_GROUP) * group_keys)
                    acc_ref[:, group_lanes(hd)] += _dot(w_ref[:, cols],
                                                        stacked_values(item_start(g), group_keys, hd))

            for hd in range(n_heads):
                z_ref[hd] = scores(qh[hd], hd, item_start(0), group_keys)
            w_ref[...] = jnp.zeros_like(w_ref)

            def key_group(state):
                g, _ = state
                pv(g - 1)
                no_mask = [None] * group
                suffixes = suffix_sums(softplus_keys(z_ref[0], no_mask), group_widths)
                for hd in range(n_heads):
                    ahead = (suffix_sums(softplus_keys(z_ref[hd + 1], no_mask), group_widths)
                             if hd + 1 < n_heads else None)
                    w_head = weights_from(z_ref[hd], suffixes, hd, qrows, group_widths, no_mask)
                    w_ref[:, hd * group_keys:(hd + 1) * group_keys] = jnp.concatenate(w_head, axis=1)
                    z_ref[hd] = scores(qh[hd], hd, item_start(g + 1), group_keys)
                    suffixes = ahead
                return g + 1, weights_live()

            n_done, live = lax.while_loop(lambda state: jnp.logical_and(state[0] < qi - 1, state[1] > 0),
                                          key_group, (jnp.int32(0), jnp.int32(1)))
            pv(n_done - 1)

            @pl.when(live > 0)
            def _():
                sweep(qh, qrows, 0, tail_widths, None)

        write(row0, qrows)
        return 0

    lax.fori_loop(1, (q_ref.shape[0] - blk) // qrows, wide_block, 0)


def _suffix_matrix():
    j = jnp.arange(ATTN_SUFFIX_BLOCK)[:, None]
    s = jnp.arange(ATTN_SUFFIX_BLOCK)[None, :]
    return -(j >= s).astype(_BF16)


def _sb_attention(q, kt, v):
    b, tp, d = q.shape
    assert (tp - ATTN_BLOCK) % ATTN_Q_ROWS == 0 and (ATTN_Q_ROWS // ATTN_BLOCK) % ATTN_KEY_GROUP == 0
    lanes = ATTN_HEADS_PER_STEP * HEAD_DIM
    return pl.pallas_call(
        _attn_kernel,
        grid=(b, d // lanes),
        in_specs=[pl.BlockSpec((None, tp, lanes), lambda i, j: (i, 0, j)),
                  pl.BlockSpec((None, lanes, tp), lambda i, j: (i, j, 0)),
                  pl.BlockSpec((None, tp, lanes), lambda i, j: (i, 0, j)),
                  _resident((ATTN_SUFFIX_BLOCK, ATTN_SUFFIX_BLOCK))],
        out_specs=pl.BlockSpec((None, tp, lanes), lambda i, j: (i, 0, j)),
        out_shape=jax.ShapeDtypeStruct((b, tp, d), _BF16),
        scratch_shapes=[pltpu.VMEM((ATTN_Q_ROWS, lanes), _F32),
                        pltpu.VMEM((ATTN_HEADS_PER_STEP, ATTN_Q_ROWS, LANES), _F32),
                        pltpu.VMEM((ATTN_HEADS_PER_STEP, ATTN_Q_ROWS, ATTN_KEY_GROUP * ATTN_BLOCK), _F32),
                        pltpu.VMEM((ATTN_Q_ROWS, ATTN_HEADS_PER_STEP * ATTN_KEY_GROUP * ATTN_BLOCK), _BF16)],
        compiler_params=pltpu.CompilerParams(dimension_semantics=("parallel", "parallel"),
                                             vmem_limit_bytes=VMEM_LIMIT),
        name="sb_attn",
    )(q, kt, v, _suffix_matrix())


def _gelu_tanh(x):
    c = math.sqrt(2.0 / math.pi)
    half = 0.5 * x
    return half + half * jnp.tanh(x * ((0.044715 * c) * (x * x) + c))


def _sigmoid(x):
    return 0.5 + 0.5 * jnp.tanh(0.5 * x)


def _lru_stage(h_ref, g_ref, win_ref, cw_ref, cb_ref, wg_ref, brg_ref, big_ref, lam_ref, y_ref,
               rec_ref, a_ref, b_ref, state_ref):
    rows, d = h_ref.shape
    tail = SUBLANES

    hn = _rms_norm(h_ref[...], g_ref[...]).astype(_BF16)
    gate = _gelu_tanh(_dot(hn, win_ref[:, :d]))
    rec_ref[tail:, :] = _dot(hn, win_ref[:, d:])
    yield

    u = cb_ref[...] + cw_ref[CONV_WIDTH - 1:CONV_WIDTH, :] * rec_ref[tail:, :]
    for j in range(CONV_WIDTH - 1):
        shift = CONV_WIDTH - 1 - j
        u = u + cw_ref[j:j + 1, :] * rec_ref[tail - shift:tail - shift + rows, :]
    rec_ref[0:tail, :] = rec_ref[rows:rows + tail, :]
    yield

    neg_lam = -lam_ref[...]
    log_a_unit = -LRU_C * (jnp.maximum(neg_lam, 0.0) + jnp.log1p(jnp.exp(-jnp.abs(neg_lam))))
    ub = u.astype(_BF16)
    for n in range(LRU_BLOCKS):
        cols = slice(n * LRU_BLOCK_DIM, (n + 1) * LRU_BLOCK_DIM)
        ri = _dot(ub[:, cols], wg_ref[n])
        r = _sigmoid(ri[:, :LRU_BLOCK_DIM] + brg_ref[:, cols])
        i = _sigmoid(ri[:, LRU_BLOCK_DIM:] + big_ref[:, cols])
        log_a = log_a_unit[:, cols] * r
        a = jnp.exp(log_a)
        one_minus_a2 = -jnp.tanh(log_a) * (1.0 + a * a)
        mult = jnp.where(one_minus_a2 > 0.0, one_minus_a2 * lax.rsqrt(one_minus_a2), 0.0)
        a_ref[:, cols] = a
        b_ref[:, cols] = mult * (i * u[:, cols])
        yield

    sub = lax.broadcasted_iota(jnp.int32, (SUBLANES, d), 0)
    state = state_ref[...]
    for gidx in range(rows // SUBLANES):
        r0 = gidx * SUBLANES
        av = a_ref[r0:r0 + SUBLANES, :]
        bv = b_ref[r0:r0 + SUBLANES, :]
        for s in (1, 2, 4):
            keep = sub >= s
            bv = jnp.where(keep, av * pltpu.roll(bv, s, axis=0) + bv, bv)
            av = jnp.where(keep, av * pltpu.roll(av, s, axis=0), av)
        hv = av * state + bv
        b_ref[r0:r0 + SUBLANES, :] = hv
        state = jnp.broadcast_to(hv[SUBLANES - 1:SUBLANES, :], (SUBLANES, d))
        if gidx % LRU_SCAN_GROUPS_PER_PIECE == LRU_SCAN_GROUPS_PER_PIECE - 1:
            yield
    state_ref[...] = state
    y_ref[...] = (b_ref[...] * gate).astype(y_ref.dtype)


def _mlp_stage(y, h, wo_ref, g_ref, wup_ref, wdn_ref, gf_ref, o_ref, hn_ref, final_norm, ff_chunk):
    d_ff = wup_ref.shape[1]
    h1 = h + _dot(y, wo_ref[...])
    hn_ref[...] = _rms_norm(h1, g_ref[...]).astype(_BF16)
    o_ref[...] = h1
    yield
    def hidden(c):
        up = jnp.maximum(_dot(hn_ref[...], wup_ref[:, c:c + ff_chunk]), 0.0)
        return (up * up).astype(_BF16)

    up = hidden(0)
    for c in range(0, d_ff, ff_chunk):
        yield
        up_ahead = hidden(c + ff_chunk) if c + ff_chunk < d_ff else None
        yield
        o_ref[...] += _dot(up, wdn_ref[c:c + ff_chunk, :])
        up = up_ahead
    yield
    if final_norm:
        o_ref[...] = _rms_norm(o_ref[...], gf_ref[...])


def _interleave(*stages):
    live = list(stages)
    while live:
        for stage in list(live):
            if next(stage, StopIteration) is StopIteration:
                live.remove(stage)


def _mlp_kernel(y_ref, h_ref, wo_ref, g_ref, wup_ref, wdn_ref, gf_ref, o_ref, hn_ref, *, final_norm):
    _interleave(_mlp_stage(y_ref[...], h_ref[...], wo_ref, g_ref, wup_ref, wdn_ref, gf_ref, o_ref, hn_ref,
                           final_norm, MLP_FF_CHUNK))


def _proj_mlp(y, h, w_o, g, w_up, w_down, g_final, final_norm):
    m, d = h.shape
    d_ff = w_up.shape[1]
    row_spec = pl.BlockSpec((MLP_ROWS, d), lambda i: (i, 0))
    return pl.pallas_call(
        functools.partial(_mlp_kernel, final_norm=final_norm),
        grid=(m // MLP_ROWS,),
        in_specs=[row_spec, row_spec, _resident((d, d)), _resident((1, d)), _resident((d, d_ff)),
                  _resident((d_ff, d)), _resident((1, d))],
        out_specs=row_spec,
        out_shape=jax.ShapeDtypeStruct((m, d), _F32),
        scratch_shapes=[pltpu.VMEM((MLP_ROWS, d), _BF16)],
        compiler_params=pltpu.CompilerParams(dimension_semantics=("parallel",), vmem_limit_bytes=VMEM_LIMIT),
        name="proj_mlp",
    )(y, h, w_o, g, w_up, w_down, g_final)


def _lru_layer_kernel(hcur_ref, hprev_ref, g_ref, win_ref, cw_ref, cb_ref, wg_ref, brg_ref, big_ref, lam_ref,
                      wo_ref, gm_ref, wup_ref, wdn_ref, gf_ref, out_ref,
                      rec_ref, a_ref, b_ref, state_ref, y_ref, hn_ref, obuf_ref, *, steps_per_batch, n_blocks):
    s = pl.program_id(0)
    rows = hcur_ref.shape[0]
    cur, prev = s % 2, 1 - s % 2

    @pl.when(s == 0)
    def _():
        y_ref[...] = jnp.zeros_like(y_ref)
        obuf_ref[...] = jnp.zeros_like(obuf_ref)

    @pl.when(s % steps_per_batch == 0)
    def _():
        rec_ref[0:SUBLANES, :] = jnp.zeros((SUBLANES, rec_ref.shape[1]), _F32)
        state_ref[...] = jnp.zeros_like(state_ref)

    @pl.when(s <= n_blocks)
    def _():
        y_prev = y_ref[...]
        _interleave(
            _lru_stage(hcur_ref, g_ref, win_ref, cw_ref, cb_ref, wg_ref, brg_ref, big_ref, lam_ref, y_ref,
                       rec_ref, a_ref, b_ref, state_ref),
            _mlp_stage(y_prev, hprev_ref[...], wo_ref, gm_ref, wup_ref, wdn_ref, gf_ref,
                       obuf_ref.at[cur], hn_ref, True, LRU_LAYER_FF_CHUNK))

    out_ref[0:rows - N_META, :] = obuf_ref[prev, N_META:rows, :]
    out_ref[rows - N_META:rows, :] = obuf_ref[cur, 0:N_META, :]


def _lru_layer(h, batch, seq, g, w_in, conv_w, conv_b, w_gates, b_rg, b_ig, lam, w_out, g_mlp, w_up, w_down,
               g_final):
    m, d = h.shape
    d_ff = w_up.shape[1]
    n_blocks = m // TIME_BLOCK
    steps_per_batch = n_blocks // batch
    cur_spec = pl.BlockSpec((TIME_BLOCK, d), lambda s: (jnp.minimum(s, n_blocks - 1), 0))
    prev_spec = pl.BlockSpec((TIME_BLOCK, d), lambda s: (jnp.clip(s - 1, 0, n_blocks - 1), 0))

    def out_index(s):
        block = jnp.maximum(s - 2, 0)
        return block // steps_per_batch, block % steps_per_batch, 0

    return pl.pallas_call(
        functools.partial(_lru_layer_kernel, steps_per_batch=steps_per_batch, n_blocks=n_blocks),
        grid=(n_blocks + 2,),
        in_specs=[cur_spec, prev_spec, _resident((1, d)), _resident((d, 2 * d)), _resident((CONV_WIDTH, d)),
                  _resident((1, d)), _resident((LRU_BLOCKS, LRU_BLOCK_DIM, 2 * LRU_BLOCK_DIM)),
                  _resident((1, d)), _resident((1, d)), _resident((1, d)),
                  _resident((d, d)), _resident((1, d)), _resident((d, d_ff)), _resident((d_ff, d)),
                  _resident((1, d))],
        out_specs=pl.BlockSpec((None, TIME_BLOCK, d), out_index),
        out_shape=jax.ShapeDtypeStruct((batch, seq, d), _F32),
        scratch_shapes=[pltpu.VMEM((SUBLANES + TIME_BLOCK, d), _F32), pltpu.VMEM((TIME_BLOCK, d), _F32),
                        pltpu.VMEM((TIME_BLOCK, d), _F32), pltpu.VMEM((SUBLANES, d), _F32),
                        pltpu.VMEM((TIME_BLOCK, d), _BF16), pltpu.VMEM((TIME_BLOCK, d), _BF16),
                        pltpu.VMEM((2, TIME_BLOCK, d), _F32)],
        compiler_params=pltpu.CompilerParams(dimension_semantics=("arbitrary",), vmem_limit_bytes=VMEM_LIMIT),
        name="lru_layer",
    )(h, h, g, w_in, conv_w, conv_b, w_gates, b_rg, b_ig, lam, w_out, g_mlp, w_up, w_down, g_final)


def kernel(x, meta_tokens, norm_mix, norm_mlp, sb_w_qkv, sb_w_o, lru_w_in, lru_conv_w, lru_conv_b, lru_w_rg,
           lru_b_rg, lru_w_ig, lru_b_ig, lru_lambda, lru_w_out, mlp_w_up, mlp_w_down, norm_final):
    b, seq, d = x.shape
    assert d == HEADS * HEAD_DIM == LRU_BLOCKS * LRU_BLOCK_DIM
    t_len = N_META + seq
    tp = -(-t_len // TIME_BLOCK) * TIME_BLOCK
    assert (b * tp) % MLP_ROWS == 0 and tp % ATTN_BLOCK == 0

    row = lambda v: v.reshape(1, d)
    q, kt, v, h = _qkv(x, meta_tokens.astype(x.dtype), tp, row(norm_mix[0]), sb_w_qkv[0])
    o = _sb_attention(q, kt, v)
    h = _proj_mlp(o.reshape(b * tp, d), h.reshape(b * tp, d), sb_w_o[0].astype(_BF16), row(norm_mlp[0]),
                  mlp_w_up[0].astype(_BF16), mlp_w_down[0].astype(_BF16), row(norm_final), False)

    w_gates = jnp.concatenate([lru_w_rg[0], lru_w_ig[0]], axis=-1).astype(_BF16)
    return _lru_layer(h, b, seq, row(norm_mix[1]), lru_w_in[0].astype(_BF16), lru_conv_w[0],
                      row(lru_conv_b[0]), w_gates, row(lru_b_rg[0]), row(lru_b_ig[0]), row(lru_lambda[0]),
                      lru_w_out[0].astype(_BF16), row(norm_mlp[1]), mlp_w_up[1].astype(_BF16),
                      mlp_w_down[1].astype(_BF16), row(norm_final))
```

```python
import functools
import math

import jax
import jax.numpy as jnp
from jax import lax
from jax.experimental import pallas as pl
from jax.experimental.pallas import tpu as pltpu

N_META = 16
HEADS = 16
HEAD_DIM = 64
LRU_BLOCKS = 8
LRU_BLOCK_DIM = 128
CONV_WIDTH = 4
LRU_C = 8.0
EPS = 1e-6
LOG2E = 1.4426950408889634

LANES = 128
SUBLANES = 8
TIME_BLOCK = 384
ATTN_BLOCK = 128
ATTN_Q_ROWS = 256
ATTN_KEY_GROUP = 2
ATTN_SUFFIX_BLOCK = 256
ATTN_HEAD_GROUP = 4
ATTN_HEADS_PER_STEP = 8
ATTN_LOG2_WEIGHT_FLOOR = -192.0
MLP_ROWS = 512
MLP_FF_CHUNK = 1024
LRU_SCAN_GROUPS_PER_PIECE = 8
LRU_LAYER_FF_CHUNK = 512
VMEM_LIMIT = 56 * 1024 * 1024

_BF16 = jnp.bfloat16
_F32 = jnp.float32


def _dot(a, b):
    return jnp.dot(a, b, preferred_element_type=_F32)


def _rms_norm(x, g):
    ms = jnp.mean(x * x, axis=-1, keepdims=True)
    return (x * lax.rsqrt(ms + EPS)) * g


def _resident(shape):
    zeros = (0,) * len(shape)
    return pl.BlockSpec(shape, lambda *_: zeros, pipeline_mode=pl.Buffered(1))


def _qkv_kernel(x_ref, meta_ref, g_ref, w_ref, q_ref, kt_ref, v_ref, h_ref, *, pad_rows):
    j, last = pl.program_id(1), pl.num_programs(1) - 1
    rows, d = x_ref.shape
    x = x_ref[...]
    first_block = jnp.concatenate([meta_ref[...], x[:rows - N_META]], axis=0)
    last_block = jnp.concatenate([x[pad_rows:], jnp.zeros((pad_rows, d), x.dtype)], axis=0)
    h = jnp.where(j == 0, first_block, jnp.where(j == last, last_block, x))
    h_ref[...] = h
    hn = _rms_norm(h, g_ref[...]).astype(_BF16)
    q_ref[...] = (_dot(hn, w_ref[:, :d].astype(_BF16)) * (HEAD_DIM ** -0.5 * LOG2E)).astype(_BF16)
    v_ref[...] = _dot(hn, w_ref[:, 2 * d:].astype(_BF16)).astype(_BF16)
    kt = lax.dot_general(w_ref[:, d:2 * d].astype(_BF16), hn, (((0,), (1,)), ((), ())),
                         preferred_element_type=_F32)
    kt_ref[...] = kt.astype(_BF16)


def _qkv(x, meta, tp, g, w_qkv):
    b, seq, d = x.shape
    nt = tp // TIME_BLOCK
    pad_rows = tp - N_META - seq
    assert nt >= 2 and seq >= TIME_BLOCK and all(n % SUBLANES == 0 for n in (pad_rows, N_META, seq, TIME_BLOCK))
    row_spec = pl.BlockSpec((None, TIME_BLOCK, d), lambda i, j: (i, j, 0))
    tiles, meta_tiles = TIME_BLOCK // SUBLANES, N_META // SUBLANES
    x_spec = pl.BlockSpec(
        (pl.Squeezed(), pl.Element(TIME_BLOCK), pl.Element(d)),
        lambda i, j: (i, jnp.clip(j * tiles - meta_tiles, 0, (seq - TIME_BLOCK) // SUBLANES) * SUBLANES, 0))
    return pl.pallas_call(
        functools.partial(_qkv_kernel, pad_rows=pad_rows),
        grid=(b, nt),
        in_specs=[x_spec, _resident((N_META, d)), _resident((1, d)), _resident((d, 3 * d))],
        out_specs=[row_spec, pl.BlockSpec((None, d, TIME_BLOCK), lambda i, j: (i, 0, j)), row_spec, row_spec],
        out_shape=[jax.ShapeDtypeStruct((b, tp, d), _BF16), jax.ShapeDtypeStruct((b, d, tp), _BF16),
                   jax.ShapeDtypeStruct((b, tp, d), _BF16), jax.ShapeDtypeStruct((b, tp, d), _F32)],
        compiler_params=pltpu.CompilerParams(dimension_semantics=("parallel", "parallel"),
                                             vmem_limit_bytes=VMEM_LIMIT),
        name="qkv",
    )(x, meta, g, w_qkv)


def _attn_kernel(q_ref, kt_ref, v_ref, nu_ref, o_ref, acc_ref, carry_ref, z_ref, w_ref):
    blk, qrows, group = ATTN_BLOCK, ATTN_Q_ROWS, ATTN_KEY_GROUP
    n_heads = q_ref.shape[1] // HEAD_DIM
    lanes = ATTN_HEAD_GROUP * HEAD_DIM

    def group_lanes(hd):
        lo = hd // ATTN_HEAD_GROUP * lanes
        return slice(lo, lo + lanes)

    def head_of_lane(rows):
        return lax.broadcasted_iota(jnp.int32, (rows, lanes), 1) // HEAD_DIM

    def stacked_values(key_blk, keys, first_head):
        v = v_ref[pl.ds(pl.multiple_of(key_blk * blk, blk), keys), group_lanes(first_head)]
        v_head = head_of_lane(keys)
        return jnp.concatenate([jnp.where(v_head == hd, v, jnp.zeros_like(v)) for hd in range(ATTN_HEAD_GROUP)],
                               axis=0)

    def scores(q_head, hd, key_blk, keys):
        return _dot(q_head, kt_ref[group_lanes(hd), pl.ds(pl.multiple_of(key_blk * blk, blk), keys)])

    def visibility(rows, n_blk, causal_shift):
        masks = []
        for k_idx in range(n_blk):
            if causal_shift is None or (k_idx + 1) * blk - causal_shift <= 0:
                masks.append(None)
            else:
                key_pos = lax.broadcasted_iota(jnp.int32, (rows, blk), 1) + (k_idx * blk - causal_shift)
                masks.append(key_pos < lax.broadcasted_iota(jnp.int32, (rows, blk), 0))
        return masks

    def softplus_keys(z, masks):
        sp = jnp.maximum(jnp.log2(1.0 + jnp.exp2(jnp.minimum(z, 100.0))), z)
        return jnp.concatenate([sp[:, k * blk:(k + 1) * blk] if m is None else
                                jnp.where(m, sp[:, k * blk:(k + 1) * blk], 0.0)
                                for k, m in enumerate(masks)], axis=1).astype(_BF16)

    def suffix_sums(sp, widths):
        out, lo = [], 0
        for width in widths:
            out.append(_dot(sp[:, lo:lo + width], nu_ref[:width, :width]))
            lo += width
        return out

    def weights_from(z, suffixes, hd, rows, widths, masks):
        carry = carry_ref[hd, :rows, :]
        w_head = [None] * len(masks)
        lo = sum(widths)
        for width, suffix in zip(reversed(widths), reversed(suffixes)):
            lo -= width
            for c in range(width // blk):
                k_idx = lo // blk + c
                w = jnp.exp2(z[:, k_idx * blk:(k_idx + 1) * blk] + suffix[:, c * blk:(c + 1) * blk] + carry)
                if masks[k_idx] is not None:
                    w = jnp.where(masks[k_idx], w, 0.0)
                w_head[k_idx] = w.astype(_BF16)
            carry = carry + jnp.broadcast_to(suffix[:, 0:1], (rows, LANES))
        carry_ref[hd, :rows, :] = carry
        return w_head

    def sweep(qh, rows, key_blk, widths, causal_shift):
        keys = sum(widths)
        masks = visibility(rows, keys // blk, causal_shift)
        v = v_ref[pl.ds(pl.multiple_of(key_blk * blk, blk), keys), :]
        v_head = head_of_lane(keys)
        zs, suffixes = {}, {}
        for tick in range(n_heads + 2):
            if tick < n_heads:
                zs[tick] = scores(qh[tick], tick, key_blk, keys)
            if 0 <= tick - 1 < n_heads:
                suffixes[tick - 1] = suffix_sums(softplus_keys(zs[tick - 1], masks), widths)
            if 0 <= tick - 2 < n_heads:
                hd = tick - 2
                w = jnp.concatenate(weights_from(zs.pop(hd), suffixes.pop(hd), hd, rows, widths, masks), axis=1)
                v_group = v[:, group_lanes(hd)]
                acc_ref[:rows, group_lanes(hd)] += _dot(
                    w, jnp.where(v_head == hd % ATTN_HEAD_GROUP, v_group, jnp.zeros_like(v_group)))

    def start_block(row0, rows):
        q = q_ref[pl.ds(row0, rows), :]
        q_head = head_of_lane(rows)
        acc_ref[...] = jnp.zeros_like(acc_ref)
        carry_ref[...] = jnp.zeros_like(carry_ref)
        groups = [q[:, group_lanes(hd)] for hd in range(0, n_heads, ATTN_HEAD_GROUP)]
        return [jnp.where(q_head == hd % ATTN_HEAD_GROUP, groups[hd // ATTN_HEAD_GROUP],
                          jnp.zeros_like(groups[0])) for hd in range(n_heads)]

    def write(row0, rows):
        o_ref[pl.ds(row0, rows), :] = acc_ref[:rows, :].astype(o_ref.dtype)

    def weights_live():
        return (jnp.max(carry_ref[...]) > ATTN_LOG2_WEIGHT_FLOOR).astype(jnp.int32)

    wide = ATTN_SUFFIX_BLOCK
    group_widths = [wide] * (group * blk // wide)
    group_keys = group * blk
    near = wide // blk
    tail_blocks = 1 + group - near
    tail_widths = [blk] * (tail_blocks % near) + [wide] * (tail_blocks // near)
    assert group_keys == qrows and 0 < near <= group

    sweep(start_block(0, blk), blk, 0, [blk], 0)
    write(0, blk)
    sweep(start_block(blk, qrows), qrows, 0, [blk] + group_widths, blk)
    write(blk, qrows)

    def wide_block(qi, _):
        row0 = pl.multiple_of(blk + qi * qrows, blk)
        first = 1 + qi * group
        qh = start_block(row0, qrows)
        sweep(qh, qrows, first - near, [wide] + group_widths, wide)

        @pl.when(weights_live() > 0)
        def _():
            def item_start(g):
                return jnp.maximum(first - near - (g + 1) * group, 0)

            def pv(g):
                for hd in range(0, n_heads, ATTN_HEAD_GROUP):
                    cols = slice(hd * group_keys, (hd + ATTN_HEAD_GROUP) * group_keys)
                    acc_ref[:, group_lanes(hd)] += _dot(w_ref[:, cols],
                                                        stacked_values(item_start(g), group_keys, hd))

            for hd in range(n_heads):
                z_ref[hd] = scores(qh[hd], hd, item_start(0), group_keys)
            w_ref[...] = jnp.zeros_like(w_ref)

            def key_group(state):
                g, _ = state
                pv(g - 1)
                no_mask = [None] * group
                suffixes = suffix_sums(softplus_keys(z_ref[0], no_mask), group_widths)
                for hd in range(n_heads):
                    ahead = (suffix_sums(softplus_keys(z_ref[hd + 1], no_mask), group_widths)
                             if hd + 1 < n_heads else None)
                    w_head = weights_from(z_ref[hd], suffixes, hd, qrows, group_widths, no_mask)
                    w_ref[:, hd * group_keys:(hd + 1) * group_keys] = jnp.concatenate(w_head, axis=1)
                    z_ref[hd] = scores(qh[hd], hd, item_start(g + 1), group_keys)
                    suffixes = ahead
                return g + 1, weights_live()

            n_done, live = lax.while_loop(lambda state: jnp.logical_and(state[0] < qi - 1, state[1] > 0),
                                          key_group, (jnp.int32(0), jnp.int32(1)))
            pv(n_done - 1)

            @pl.when(live > 0)
            def _():
                sweep(qh, qrows, 0, tail_widths, None)

        write(row0, qrows)
        return 0

    lax.fori_loop(1, (q_ref.shape[0] - blk) // qrows, wide_block, 0)


def _suffix_matrix():
    j = jnp.arange(ATTN_SUFFIX_BLOCK)[:, None]
    s = jnp.arange(ATTN_SUFFIX_BLOCK)[None, :]
    return -(j >= s).astype(_BF16)


def _sb_attention(q, kt, v):
    b, tp, d = q.shape
    assert (tp - ATTN_BLOCK) % ATTN_Q_ROWS == 0 and (ATTN_Q_ROWS // ATTN_BLOCK) % ATTN_KEY_GROUP == 0
    lanes = ATTN_HEADS_PER_STEP * HEAD_DIM
    return pl.pallas_call(
        _attn_kernel,
        grid=(b, d // lanes),
        in_specs=[pl.BlockSpec((None, tp, lanes), lambda i, j: (i, 0, j)),
                  pl.BlockSpec((None, lanes, tp), lambda i, j: (i, j, 0)),
                  pl.BlockSpec((None, tp, lanes), lambda i, j: (i, 0, j)),
                  _resident((ATTN_SUFFIX_BLOCK, ATTN_SUFFIX_BLOCK))],
        out_specs=pl.BlockSpec((None, tp, lanes), lambda i, j: (i, 0, j)),
        out_shape=jax.ShapeDtypeStruct((b, tp, d), _BF16),
        scratch_shapes=[pltpu.VMEM((ATTN_Q_ROWS, lanes), _F32),
                        pltpu.VMEM((ATTN_HEADS_PER_STEP, ATTN_Q_ROWS, LANES), _F32),
                        pltpu.VMEM((ATTN_HEADS_PER_STEP, ATTN_Q_ROWS, ATTN_KEY_GROUP * ATTN_BLOCK), _F32),
                        pltpu.VMEM((ATTN_Q_ROWS, ATTN_HEADS_PER_STEP * ATTN_KEY_GROUP * ATTN_BLOCK), _BF16)],
        compiler_params=pltpu.CompilerParams(dimension_semantics=("parallel", "parallel"),
                                             vmem_limit_bytes=VMEM_LIMIT),
        name="sb_attn",
    )(q, kt, v, _suffix_matrix())


def _gelu_tanh(x):
    c = math.sqrt(2.0 / math.pi)
    half = 0.5 * x
    return half + half * jnp.tanh(x * ((0.044715 * c) * (x * x) + c))


def _sigmoid(x):
    return 0.5 + 0.5 * jnp.tanh(0.5 * x)


def _lru_stage(h_ref, g_ref, win_ref, cw_ref, cb_ref, wg_ref, brg_ref, big_ref, lam_ref, y_ref,
               rec_ref, a_ref, b_ref, state_ref):
    rows, d = h_ref.shape
    tail = SUBLANES

    hn = _rms_norm(h_ref[...], g_ref[...]).astype(_BF16)
    gate = _gelu_tanh(_dot(hn, win_ref[:, :d]))
    rec_ref[tail:, :] = _dot(hn, win_ref[:, d:])
    yield

    u = cb_ref[...] + cw_ref[CONV_WIDTH - 1:CONV_WIDTH, :] * rec_ref[tail:, :]
    for j in range(CONV_WIDTH - 1):
        shift = CONV_WIDTH - 1 - j
        u = u + cw_ref[j:j + 1, :] * rec_ref[tail - shift:tail - shift + rows, :]
    rec_ref[0:tail, :] = rec_ref[rows:rows + tail, :]
    yield

    neg_lam = -lam_ref[...]
    log_a_unit = -LRU_C * (jnp.maximum(neg_lam, 0.0) + jnp.log1p(jnp.exp(-jnp.abs(neg_lam))))
    ub = u.astype(_BF16)
    for n in range(LRU_BLOCKS):
        cols = slice(n * LRU_BLOCK_DIM, (n + 1) * LRU_BLOCK_DIM)
        ri = _dot(ub[:, cols], wg_ref[n])
        r = _sigmoid(ri[:, :LRU_BLOCK_DIM] + brg_ref[:, cols])
        i = _sigmoid(ri[:, LRU_BLOCK_DIM:] + big_ref[:, cols])
        log_a = log_a_unit[:, cols] * r
        a = jnp.exp(log_a)
        one_minus_a2 = -jnp.tanh(log_a) * (1.0 + a * a)
        mult = jnp.where(one_minus_a2 > 0.0, one_minus_a2 * lax.rsqrt(one_minus_a2), 0.0)
        a_ref[:, cols] = a
        b_ref[:, cols] = mult * (i * u[:, cols])
        yield

    sub = lax.broadcasted_iota(jnp.int32, (SUBLANES, d), 0)
    state = state_ref[...]
    for gidx in range(rows // SUBLANES):
        r0 = gidx * SUBLANES
        av = a_ref[r0:r0 + SUBLANES, :]
        bv = b_ref[r0:r0 + SUBLANES, :]
        for s in (1, 2, 4):
            keep = sub >= s
            bv = jnp.where(keep, av * pltpu.roll(bv, s, axis=0) + bv, bv)
            av = jnp.where(keep, av * pltpu.roll(av, s, axis=0), av)
        hv = av * state + bv
        b_ref[r0:r0 + SUBLANES, :] = hv
        state = jnp.broadcast_to(hv[SUBLANES - 1:SUBLANES, :], (SUBLANES, d))
        if gidx % LRU_SCAN_GROUPS_PER_PIECE == LRU_SCAN_GROUPS_PER_PIECE - 1:
            yield
    state_ref[...] = state
    y_ref[...] = (b_ref[...] * gate).astype(y_ref.dtype)


def _mlp_stage(y, h, wo_ref, g_ref, wup_ref, wdn_ref, gf_ref, o_ref, hn_ref, final_norm, ff_chunk):
    d_ff = wup_ref.shape[1]
    h1 = h + _dot(y, wo_ref[...])
    hn_ref[...] = _rms_norm(h1, g_ref[...]).astype(_BF16)
    o_ref[...] = h1
    yield
    for c in range(0, d_ff, ff_chunk):
        up = jnp.maximum(_dot(hn_ref[...], wup_ref[:, c:c + ff_chunk]), 0.0)
        up = (up * up).astype(_BF16)
        yield
        o_ref[...] += _dot(up, wdn_ref[c:c + ff_chunk, :])
        yield
    if final_norm:
        o_ref[...] = _rms_norm(o_ref[...], gf_ref[...])


def _interleave(*stages):
    live = list(stages)
    while live:
        for stage in list(live):
            if next(stage, StopIteration) is StopIteration:
                live.remove(stage)


def _mlp_kernel(y_ref, h_ref, wo_ref, g_ref, wup_ref, wdn_ref, gf_ref, o_ref, hn_ref, *, final_norm):
    _interleave(_mlp_stage(y_ref[...], h_ref[...], wo_ref, g_ref, wup_ref, wdn_ref, gf_ref, o_ref, hn_ref,
                           final_norm, MLP_FF_CHUNK))


def _proj_mlp(y, h, w_o, g, w_up, w_down, g_final, final_norm):
    m, d = h.shape
    d_ff = w_up.shape[1]
    row_spec = pl.BlockSpec((MLP_ROWS, d), lambda i: (i, 0))
    return pl.pallas_call(
        functools.partial(_mlp_kernel, final_norm=final_norm),
        grid=(m // MLP_ROWS,),
        in_specs=[row_spec, row_spec, _resident((d, d)), _resident((1, d)), _resident((d, d_ff)),
                  _resident((d_ff, d)), _resident((1, d))],
        out_specs=row_spec,
        out_shape=jax.ShapeDtypeStruct((m, d), _F32),
        scratch_shapes=[pltpu.VMEM((MLP_ROWS, d), _BF16)],
        compiler_params=pltpu.CompilerParams(dimension_semantics=("parallel",), vmem_limit_bytes=VMEM_LIMIT),
        name="proj_mlp",
    )(y, h, w_o, g, w_up, w_down, g_final)


def _lru_layer_kernel(hcur_ref, hprev_ref, g_ref, win_ref, cw_ref, cb_ref, wg_ref, brg_ref, big_ref, lam_ref,
                      wo_ref, gm_ref, wup_ref, wdn_ref, gf_ref, out_ref,
                      rec_ref, a_ref, b_ref, state_ref, y_ref, hn_ref, obuf_ref, *, steps_per_batch, n_blocks):
    s = pl.program_id(0)
    rows = hcur_ref.shape[0]
    cur, prev = s % 2, 1 - s % 2

    @pl.when(s == 0)
    def _():
        obuf_ref[...] = jnp.zeros_like(obuf_ref)

    @pl.when(s % steps_per_batch == 0)
    def _():
        rec_ref[0:SUBLANES, :] = jnp.zeros((SUBLANES, rec_ref.shape[1]), _F32)
        state_ref[...] = jnp.zeros_like(state_ref)

    def mixer():
        return _lru_stage(hcur_ref, g_ref, win_ref, cw_ref, cb_ref, wg_ref, brg_ref, big_ref, lam_ref, y_ref,
                          rec_ref, a_ref, b_ref, state_ref)

    def mlp():
        return _mlp_stage(y_ref[...], hprev_ref[...], wo_ref, gm_ref, wup_ref, wdn_ref, gf_ref,
                          obuf_ref.at[cur], hn_ref, True, LRU_LAYER_FF_CHUNK)

    @pl.when(s == 0)
    def _():
        _interleave(mixer())

    @pl.when(jnp.logical_and(s >= 1, s < n_blocks))
    def _():
        _interleave(mixer(), mlp())

    @pl.when(s == n_blocks)
    def _():
        _interleave(mlp())

    out_ref[0:rows - N_META, :] = obuf_ref[prev, N_META:rows, :]
    out_ref[rows - N_META:rows, :] = obuf_ref[cur, 0:N_META, :]


def _lru_layer(h, batch, seq, g, w_in, conv_w, conv_b, w_gates, b_rg, b_ig, lam, w_out, g_mlp, w_up, w_down,
               g_final):
    m, d = h.shape
    d_ff = w_up.shape[1]
    n_blocks = m // TIME_BLOCK
    steps_per_batch = n_blocks // batch
    cur_spec = pl.BlockSpec((TIME_BLOCK, d), lambda s: (jnp.minimum(s, n_blocks - 1), 0))
    prev_spec = pl.BlockSpec((TIME_BLOCK, d), lambda s: (jnp.clip(s - 1, 0, n_blocks - 1), 0))

    def out_index(s):
        block = jnp.maximum(s - 2, 0)
        return block // steps_per_batch, block % steps_per_batch, 0

    return pl.pallas_call(
        functools.partial(_lru_layer_kernel, steps_per_batch=steps_per_batch, n_blocks=n_blocks),
        grid=(n_blocks + 2,),
        in_specs=[cur_spec, prev_spec, _resident((1, d)), _resident((d, 2 * d)), _resident((CONV_WIDTH, d)),
                  _resident((1, d)), _resident((LRU_BLOCKS, LRU_BLOCK_DIM, 2 * LRU_BLOCK_DIM)),
                  _resident((1, d)), _resident((1, d)), _resident((1, d)),
                  _resident((d, d)), _resident((1, d)), _resident((d, d_ff)), _resident((d_ff, d)),
                  _resident((1, d))],
        out_specs=pl.BlockSpec((None, TIME_BLOCK, d), out_index),
        out_shape=jax.ShapeDtypeStruct((batch, seq, d), _F32),
        scratch_shapes=[pltpu.VMEM((SUBLANES + TIME_BLOCK, d), _F32), pltpu.VMEM((TIME_BLOCK, d), _F32),
                        pltpu.VMEM((TIME_BLOCK, d), _F32), pltpu.VMEM((SUBLANES, d), _F32),
                        pltpu.VMEM((TIME_BLOCK, d), _BF16), pltpu.VMEM((TIME_BLOCK, d), _BF16),
                        pltpu.VMEM((2, TIME_BLOCK, d), _F32)],
        compiler_params=pltpu.CompilerParams(dimension_semantics=("arbitrary",), vmem_limit_bytes=VMEM_LIMIT),
        name="lru_layer",
    )(h, h, g, w_in, conv_w, conv_b, w_gates, b_rg, b_ig, lam, w_out, g_mlp, w_up, w_down, g_final)


def kernel(x, meta_tokens, norm_mix, norm_mlp, sb_w_qkv, sb_w_o, lru_w_in, lru_conv_w, lru_conv_b, lru_w_rg,
           lru_b_rg, lru_w_ig, lru_b_ig, lru_lambda, lru_w_out, mlp_w_up, mlp_w_down, norm_final):
    b, seq, d = x.shape
    assert d == HEADS * HEAD_DIM == LRU_BLOCKS * LRU_BLOCK_DIM
    t_len = N_META + seq
    tp = -(-t_len // TIME_BLOCK) * TIME_BLOCK
    assert (b * tp) % MLP_ROWS == 0 and tp % ATTN_BLOCK == 0

    row = lambda v: v.reshape(1, d)
    q, kt, v, h = _qkv(x, meta_tokens.astype(x.dtype), tp, row(norm_mix[0]), sb_w_qkv[0])
    o = _sb_attention(q, kt, v)
    h = _proj_mlp(o.reshape(b * tp, d), h.reshape(b * tp, d), sb_w_o[0].astype(_BF16), row(norm_mlp[0]),
                  mlp_w_up[0].astype(_BF16), mlp_w_down[0].astype(_BF16), row(norm_final), False)

    w_gates = jnp.concatenate([lru_w_rg[0], lru_w_ig[0]], axis=-1).astype(_BF16)
    return _lru_layer(h, b, seq, row(norm_mix[1]), lru_w_in[0].astype(_BF16), lru_conv_w[0],
                      row(lru_conv_b[0]), w_gates, row(lru_b_rg[0]), row(lru_b_ig[0]), row(lru_lambda[0]),
                      lru_w_out[0].astype(_BF16), row(norm_mlp[1]), mlp_w_up[1].astype(_BF16),
                      mlp_w_down[1].astype(_BF16), row(norm_final))
```

```python
import functools
import math

import jax
import jax.numpy as jnp
from jax import lax
from jax.experimental import pallas as pl
from jax.experimental.pallas import tpu as pltpu

N_META = 16
HEADS = 16
HEAD_DIM = 64
LRU_BLOCKS = 8
LRU_BLOCK_DIM = 128
CONV_WIDTH = 4
LRU_C = 8.0
EPS = 1e-6
LOG2E = 1.4426950408889634

LANES = 128
SUBLANES = 8
TIME_BLOCK = 384
ATTN_BLOCK = 128
ATTN_Q_ROWS = 256
ATTN_KEY_GROUP = 2
ATTN_SUFFIX_BLOCK = 256
ATTN_HEAD_GROUP = 4
ATTN_HEADS_PER_STEP = 8
ATTN_LOG2_WEIGHT_FLOOR = -192.0
MLP_ROWS = 512
MLP_FF_CHUNK = 1024
LRU_SCAN_GROUPS_PER_PIECE = 8
LRU_LAYER_FF_CHUNK = 512
VMEM_LIMIT = 56 * 1024 * 1024

_BF16 = jnp.bfloat16
_F32 = jnp.float32


def _dot(a, b):
    return jnp.dot(a, b, preferred_element_type=_F32)


def _rms_norm(x, g):
    ms = jnp.mean(x * x, axis=-1, keepdims=True)
    return (x * lax.rsqrt(ms + EPS)) * g


def _resident(shape):
    zeros = (0,) * len(shape)
    return pl.BlockSpec(shape, lambda *_: zeros, pipeline_mode=pl.Buffered(1))


def _qkv_kernel(x_ref, meta_ref, g_ref, w_ref, q_ref, kt_ref, v_ref, h_ref, *, pad_rows):
    j, last = pl.program_id(1), pl.num_programs(1) - 1
    rows, d = x_ref.shape
    x = x_ref[...]
    first_block = jnp.concatenate([meta_ref[...], x[:rows - N_META]], axis=0)
    last_block = jnp.concatenate([x[pad_rows:], jnp.zeros((pad_rows, d), x.dtype)], axis=0)
    h = jnp.where(j == 0, first_block, jnp.where(j == last, last_block, x))
    h_ref[...] = h
    hn = _rms_norm(h, g_ref[...]).astype(_BF16)
    q_ref[...] = (_dot(hn, w_ref[:, :d].astype(_BF16)) * (HEAD_DIM ** -0.5 * LOG2E)).astype(_BF16)
    v_ref[...] = _dot(hn, w_ref[:, 2 * d:].astype(_BF16)).astype(_BF16)
    kt = lax.dot_general(w_ref[:, d:2 * d].astype(_BF16), hn, (((0,), (1,)), ((), ())),
                         preferred_element_type=_F32)
    kt_ref[...] = kt.astype(_BF16)


def _qkv(x, meta, tp, g, w_qkv):
    b, seq, d = x.shape
    nt = tp // TIME_BLOCK
    pad_rows = tp - N_META - seq
    assert nt >= 2 and seq >= TIME_BLOCK and all(n % SUBLANES == 0 for n in (pad_rows, N_META, seq, TIME_BLOCK))
    row_spec = pl.BlockSpec((None, TIME_BLOCK, d), lambda i, j: (i, j, 0))
    tiles, meta_tiles = TIME_BLOCK // SUBLANES, N_META // SUBLANES
    x_spec = pl.BlockSpec(
        (pl.Squeezed(), pl.Element(TIME_BLOCK), pl.Element(d)),
        lambda i, j: (i, jnp.clip(j * tiles - meta_tiles, 0, (seq - TIME_BLOCK) // SUBLANES) * SUBLANES, 0))
    return pl.pallas_call(
        functools.partial(_qkv_kernel, pad_rows=pad_rows),
        grid=(b, nt),
        in_specs=[x_spec, _resident((N_META, d)), _resident((1, d)), _resident((d, 3 * d))],
        out_specs=[row_spec, pl.BlockSpec((None, d, TIME_BLOCK), lambda i, j: (i, 0, j)), row_spec, row_spec],
        out_shape=[jax.ShapeDtypeStruct((b, tp, d), _BF16), jax.ShapeDtypeStruct((b, d, tp), _BF16),
                   jax.ShapeDtypeStruct((b, tp, d), _BF16), jax.ShapeDtypeStruct((b, tp, d), _F32)],
        compiler_params=pltpu.CompilerParams(dimension_semantics=("parallel", "parallel"),
                                             vmem_limit_bytes=VMEM_LIMIT),
        name="qkv",
    )(x, meta, g, w_qkv)


def _attn_kernel(q_ref, kt_ref, v_ref, nu_ref, o_ref, acc_ref, carry_ref, z_ref, w_ref):
    blk, qrows, group = ATTN_BLOCK, ATTN_Q_ROWS, ATTN_KEY_GROUP
    n_heads = q_ref.shape[1] // HEAD_DIM
    lanes = ATTN_HEAD_GROUP * HEAD_DIM

    def group_lanes(hd):
        lo = hd // ATTN_HEAD_GROUP * lanes
        return slice(lo, lo + lanes)

    def head_of_lane(rows):
        return lax.broadcasted_iota(jnp.int32, (rows, lanes), 1) // HEAD_DIM

    def stacked_values(key_blk, keys, first_head):
        v = v_ref[pl.ds(pl.multiple_of(key_blk * blk, blk), keys), group_lanes(first_head)]
        v_head = head_of_lane(keys)
        return jnp.concatenate([jnp.where(v_head == hd, v, jnp.zeros_like(v)) for hd in range(ATTN_HEAD_GROUP)],
                               axis=0)

    def scores(q_head, hd, key_blk, keys):
        return _dot(q_head, kt_ref[group_lanes(hd), pl.ds(pl.multiple_of(key_blk * blk, blk), keys)])

    def visibility(rows, n_blk, causal_shift):
        masks = []
        for k_idx in range(n_blk):
            if causal_shift is None or (k_idx + 1) * blk - causal_shift <= 0:
                masks.append(None)
            else:
                key_pos = lax.broadcasted_iota(jnp.int32, (rows, blk), 1) + (k_idx * blk - causal_shift)
                masks.append(key_pos < lax.broadcasted_iota(jnp.int32, (rows, blk), 0))
        return masks

    def softplus_keys(z, masks):
        sp = jnp.maximum(jnp.log2(1.0 + jnp.exp2(jnp.minimum(z, 100.0))), z)
        sp_keys = jnp.concatenate([sp[:, k * blk:(k + 1) * blk] if m is None else
                                   jnp.where(m, sp[:, k * blk:(k + 1) * blk], 0.0)
                                   for k, m in enumerate(masks)], axis=1).astype(_BF16)
        return z - sp, sp_keys

    def suffix_sums(sp, widths):
        out, lo = [], 0
        for width in widths:
            out.append(_dot(sp[:, lo:lo + width], nu_ref[:width, :width]))
            lo += width
        return out

    def weights_from(log_sig, sp, suffixes, hd, rows, widths, masks):
        carry = carry_ref[hd, :rows, :]
        w_head = [None] * len(masks)
        lo = sum(widths)
        for width, suffix in zip(reversed(widths), reversed(suffixes)):
            lo -= width
            for c in range(width // blk):
                k_idx = lo // blk + c
                w = jnp.exp2(log_sig[:, k_idx * blk:(k_idx + 1) * blk] + suffix[:, c * blk:(c + 1) * blk] + carry)
                if masks[k_idx] is not None:
                    w = jnp.where(masks[k_idx], w, 0.0)
                w_head[k_idx] = w.astype(_BF16)
            total = suffix[:, :blk] - sp[:, lo:lo + blk].astype(_F32)
            carry = carry + jnp.broadcast_to(total[:, 0:1], (rows, LANES))
        carry_ref[hd, :rows, :] = carry
        return w_head

    def sweep(qh, rows, key_blk, widths, causal_shift):
        keys = sum(widths)
        masks = visibility(rows, keys // blk, causal_shift)
        v = v_ref[pl.ds(pl.multiple_of(key_blk * blk, blk), keys), :]
        v_head = head_of_lane(keys)
        zs, terms, suffixes = {}, {}, {}
        for tick in range(n_heads + 2):
            if tick < n_heads:
                zs[tick] = scores(qh[tick], tick, key_blk, keys)
            if 0 <= tick - 1 < n_heads:
                terms[tick - 1] = softplus_keys(zs.pop(tick - 1), masks)
                suffixes[tick - 1] = suffix_sums(terms[tick - 1][1], widths)
            if 0 <= tick - 2 < n_heads:
                hd = tick - 2
                w = jnp.concatenate(weights_from(*terms.pop(hd), suffixes.pop(hd), hd, rows, widths, masks), axis=1)
                v_group = v[:, group_lanes(hd)]
                acc_ref[:rows, group_lanes(hd)] += _dot(
                    w, jnp.where(v_head == hd % ATTN_HEAD_GROUP, v_group, jnp.zeros_like(v_group)))

    def start_block(row0, rows):
        q = q_ref[pl.ds(row0, rows), :]
        q_head = head_of_lane(rows)
        acc_ref[...] = jnp.zeros_like(acc_ref)
        carry_ref[...] = jnp.zeros_like(carry_ref)
        groups = [q[:, group_lanes(hd)] for hd in range(0, n_heads, ATTN_HEAD_GROUP)]
        return [jnp.where(q_head == hd % ATTN_HEAD_GROUP, groups[hd // ATTN_HEAD_GROUP],
                          jnp.zeros_like(groups[0])) for hd in range(n_heads)]

    def write(row0, rows):
        o_ref[pl.ds(row0, rows), :] = acc_ref[:rows, :].astype(o_ref.dtype)

    def weights_live():
        return (jnp.max(carry_ref[...]) > ATTN_LOG2_WEIGHT_FLOOR).astype(jnp.int32)

    wide = ATTN_SUFFIX_BLOCK
    group_widths = [wide] * (group * blk // wide)
    group_keys = group * blk
    near = wide // blk
    tail_blocks = 1 + group - near
    tail_widths = [blk] * (tail_blocks % near) + [wide] * (tail_blocks // near)
    assert group_keys == qrows and 0 < near <= group

    sweep(start_block(0, blk), blk, 0, [blk], 0)
    write(0, blk)
    sweep(start_block(blk, qrows), qrows, 0, [blk] + group_widths, blk)
    write(blk, qrows)

    def wide_block(qi, _):
        row0 = pl.multiple_of(blk + qi * qrows, blk)
        first = 1 + qi * group
        qh = start_block(row0, qrows)
        sweep(qh, qrows, first - near, [wide] + group_widths, wide)

        @pl.when(weights_live() > 0)
        def _():
            def item_start(g):
                return jnp.maximum(first - near - (g + 1) * group, 0)

            def pv(g):
                for hd in range(0, n_heads, ATTN_HEAD_GROUP):
                    cols = slice(hd * group_keys, (hd + ATTN_HEAD_GROUP) * group_keys)
                    acc_ref[:, group_lanes(hd)] += _dot(w_ref[:, cols],
                                                        stacked_values(item_start(g), group_keys, hd))

            for hd in range(n_heads):
                z_ref[hd] = scores(qh[hd], hd, item_start(0), group_keys)
            w_ref[...] = jnp.zeros_like(w_ref)

            def key_group(state):
                g, _ = state
                pv(g - 1)
                no_mask = [None] * group

                def prepare(hd):
                    log_sig, sp = softplus_keys(z_ref[hd], no_mask)
                    z_ref[hd] = log_sig
                    return sp, suffix_sums(sp, group_widths)

                ahead = prepare(0)
                for hd in range(n_heads):
                    sp, suffixes = ahead
                    if hd + 1 < n_heads:
                        ahead = prepare(hd + 1)
                    w_head = weights_from(z_ref[hd], sp, suffixes, hd, qrows, group_widths, no_mask)
                    w_ref[:, hd * group_keys:(hd + 1) * group_keys] = jnp.concatenate(w_head, axis=1)
                    z_ref[hd] = scores(qh[hd], hd, item_start(g + 1), group_keys)
                return g + 1, weights_live()

            n_done, live = lax.while_loop(lambda state: jnp.logical_and(state[0] < qi - 1, state[1] > 0),
                                          key_group, (jnp.int32(0), jnp.int32(1)))
            pv(n_done - 1)

            @pl.when(live > 0)
            def _():
                sweep(qh, qrows, 0, tail_widths, None)

        write(row0, qrows)
        return 0

    lax.fori_loop(1, (q_ref.shape[0] - blk) // qrows, wide_block, 0)


def _suffix_matrix():
    j = jnp.arange(ATTN_SUFFIX_BLOCK)[:, None]
    s = jnp.arange(ATTN_SUFFIX_BLOCK)[None, :]
    return -(j > s).astype(_BF16)


def _sb_attention(q, kt, v):
    b, tp, d = q.shape
    assert (tp - ATTN_BLOCK) % ATTN_Q_ROWS == 0 and (ATTN_Q_ROWS // ATTN_BLOCK) % ATTN_KEY_GROUP == 0
    lanes = ATTN_HEADS_PER_STEP * HEAD_DIM
    return pl.pallas_call(
        _attn_kernel,
        grid=(b, d // lanes),
        in_specs=[pl.BlockSpec((None, tp, lanes), lambda i, j: (i, 0, j)),
                  pl.BlockSpec((None, lanes, tp), lambda i, j: (i, j, 0)),
                  pl.BlockSpec((None, tp, lanes), lambda i, j: (i, 0, j)),
                  _resident((ATTN_SUFFIX_BLOCK, ATTN_SUFFIX_BLOCK))],
        out_specs=pl.BlockSpec((None, tp, lanes), lambda i, j: (i, 0, j)),
        out_shape=jax.ShapeDtypeStruct((b, tp, d), _BF16),
        scratch_shapes=[pltpu.VMEM((ATTN_Q_ROWS, lanes), _F32),
                        pltpu.VMEM((ATTN_HEADS_PER_STEP, ATTN_Q_ROWS, LANES), _F32),
                        pltpu.VMEM((ATTN_HEADS_PER_STEP, ATTN_Q_ROWS, ATTN_KEY_GROUP * ATTN_BLOCK), _F32),
                        pltpu.VMEM((ATTN_Q_ROWS, ATTN_HEADS_PER_STEP * ATTN_KEY_GROUP * ATTN_BLOCK), _BF16)],
        compiler_params=pltpu.CompilerParams(dimension_semantics=("parallel", "parallel"),
                                             vmem_limit_bytes=VMEM_LIMIT),
        name="sb_attn",
    )(q, kt, v, _suffix_matrix())


def _gelu_tanh(x):
    c = math.sqrt(2.0 / math.pi)
    half = 0.5 * x
    return half + half * jnp.tanh(x * ((0.044715 * c) * (x * x) + c))


def _sigmoid(x):
    return 0.5 + 0.5 * jnp.tanh(0.5 * x)


def _lru_stage(h_ref, g_ref, win_ref, cw_ref, cb_ref, wg_ref, brg_ref, big_ref, lam_ref, y_ref,
               rec_ref, a_ref, b_ref, state_ref):
    rows, d = h_ref.shape
    tail = SUBLANES

    hn = _rms_norm(h_ref[...], g_ref[...]).astype(_BF16)
    gate = _gelu_tanh(_dot(hn, win_ref[:, :d]))
    rec_ref[tail:, :] = _dot(hn, win_ref[:, d:])
    yield

    u = cb_ref[...] + cw_ref[CONV_WIDTH - 1:CONV_WIDTH, :] * rec_ref[tail:, :]
    for j in range(CONV_WIDTH - 1):
        shift = CONV_WIDTH - 1 - j
        u = u + cw_ref[j:j + 1, :] * rec_ref[tail - shift:tail - shift + rows, :]
    rec_ref[0:tail, :] = rec_ref[rows:rows + tail, :]
    yield

    neg_lam = -lam_ref[...]
    log_a_unit = -LRU_C * (jnp.maximum(neg_lam, 0.0) + jnp.log1p(jnp.exp(-jnp.abs(neg_lam))))
    ub = u.astype(_BF16)
    for n in range(LRU_BLOCKS):
        cols = slice(n * LRU_BLOCK_DIM, (n + 1) * LRU_BLOCK_DIM)
        ri = _dot(ub[:, cols], wg_ref[n])
        r = _sigmoid(ri[:, :LRU_BLOCK_DIM] + brg_ref[:, cols])
        i = _sigmoid(ri[:, LRU_BLOCK_DIM:] + big_ref[:, cols])
        log_a = log_a_unit[:, cols] * r
        a = jnp.exp(log_a)
        one_minus_a2 = -jnp.tanh(log_a) * (1.0 + a * a)
        mult = jnp.where(one_minus_a2 > 0.0, one_minus_a2 * lax.rsqrt(one_minus_a2), 0.0)
        a_ref[:, cols] = a
        b_ref[:, cols] = mult * (i * u[:, cols])
        yield

    sub = lax.broadcasted_iota(jnp.int32, (SUBLANES, d), 0)
    state = state_ref[...]
    for gidx in range(rows // SUBLANES):
        r0 = gidx * SUBLANES
        av = a_ref[r0:r0 + SUBLANES, :]
        bv = b_ref[r0:r0 + SUBLANES, :]
        for s in (1, 2, 4):
            keep = sub >= s
            bv = jnp.where(keep, av * pltpu.roll(bv, s, axis=0) + bv, bv)
            av = jnp.where(keep, av * pltpu.roll(av, s, axis=0), av)
        hv = av * state + bv
        b_ref[r0:r0 + SUBLANES, :] = hv
        state = jnp.broadcast_to(hv[SUBLANES - 1:SUBLANES, :], (SUBLANES, d))
        if gidx % LRU_SCAN_GROUPS_PER_PIECE == LRU_SCAN_GROUPS_PER_PIECE - 1:
            yield
    state_ref[...] = state
    y_ref[...] = (b_ref[...] * gate).astype(y_ref.dtype)


def _mlp_stage(y, h, wo_ref, g_ref, wup_ref, wdn_ref, gf_ref, o_ref, hn_ref, final_norm, ff_chunk):
    d_ff = wup_ref.shape[1]
    h1 = h + _dot(y, wo_ref[...])
    hn_ref[...] = _rms_norm(h1, g_ref[...]).astype(_BF16)
    o_ref[...] = h1
    yield
    for c in range(0, d_ff, ff_chunk):
        up = jnp.maximum(_dot(hn_ref[...], wup_ref[:, c:c + ff_chunk]), 0.0)
        up = (up * up).astype(_BF16)
        yield
        o_ref[...] += _dot(up, wdn_ref[c:c + ff_chunk, :])
        yield
    if final_norm:
        o_ref[...] = _rms_norm(o_ref[...], gf_ref[...])


def _interleave(*stages):
    live = list(stages)
    while live:
        for stage in list(live):
            if next(stage, StopIteration) is StopIteration:
                live.remove(stage)


def _mlp_kernel(y_ref, h_ref, wo_ref, g_ref, wup_ref, wdn_ref, gf_ref, o_ref, hn_ref, *, final_norm):
    _interleave(_mlp_stage(y_ref[...], h_ref[...], wo_ref, g_ref, wup_ref, wdn_ref, gf_ref, o_ref, hn_ref,
                           final_norm, MLP_FF_CHUNK))


def _proj_mlp(y, h, w_o, g, w_up, w_down, g_final, final_norm):
    m, d = h.shape
    d_ff = w_up.shape[1]
    row_spec = pl.BlockSpec((MLP_ROWS, d), lambda i: (i, 0))
    return pl.pallas_call(
        functools.partial(_mlp_kernel, final_norm=final_norm),
        grid=(m // MLP_ROWS,),
        in_specs=[row_spec, row_spec, _resident((d, d)), _resident((1, d)), _resident((d, d_ff)),
                  _resident((d_ff, d)), _resident((1, d))],
        out_specs=row_spec,
        out_shape=jax.ShapeDtypeStruct((m, d), _F32),
        scratch_shapes=[pltpu.VMEM((MLP_ROWS, d), _BF16)],
        compiler_params=pltpu.CompilerParams(dimension_semantics=("parallel",), vmem_limit_bytes=VMEM_LIMIT),
        name="proj_mlp",
    )(y, h, w_o, g, w_up, w_down, g_final)


def _lru_layer_kernel(hcur_ref, hprev_ref, g_ref, win_ref, cw_ref, cb_ref, wg_ref, brg_ref, big_ref, lam_ref,
                      wo_ref, gm_ref, wup_ref, wdn_ref, gf_ref, out_ref,
                      rec_ref, a_ref, b_ref, state_ref, y_ref, hn_ref, obuf_ref, *, steps_per_batch, n_blocks):
    s = pl.program_id(0)
    rows = hcur_ref.shape[0]
    cur, prev = s % 2, 1 - s % 2

    @pl.when(s == 0)
    def _():
        y_ref[...] = jnp.zeros_like(y_ref)
        obuf_ref[...] = jnp.zeros_like(obuf_ref)

    @pl.when(s % steps_per_batch == 0)
    def _():
        rec_ref[0:SUBLANES, :] = jnp.zeros((SUBLANES, rec_ref.shape[1]), _F32)
        state_ref[...] = jnp.zeros_like(state_ref)

    @pl.when(s <= n_blocks)
    def _():
        y_prev = y_ref[...]
        _interleave(
            _lru_stage(hcur_ref, g_ref, win_ref, cw_ref, cb_ref, wg_ref, brg_ref, big_ref, lam_ref, y_ref,
                       rec_ref, a_ref, b_ref, state_ref),
            _mlp_stage(y_prev, hprev_ref[...], wo_ref, gm_ref, wup_ref, wdn_ref, gf_ref,
                       obuf_ref.at[cur], hn_ref, True, LRU_LAYER_FF_CHUNK))

    out_ref[0:rows - N_META, :] = obuf_ref[prev, N_META:rows, :]
    out_ref[rows - N_META:rows, :] = obuf_ref[cur, 0:N_META, :]


def _lru_layer(h, batch, seq, g, w_in, conv_w, conv_b, w_gates, b_rg, b_ig, lam, w_out, g_mlp, w_up, w_down,
               g_final):
    m, d = h.shape
    d_ff = w_up.shape[1]
    n_blocks = m // TIME_BLOCK
    steps_per_batch = n_blocks // batch
    cur_spec = pl.BlockSpec((TIME_BLOCK, d), lambda s: (jnp.minimum(s, n_blocks - 1), 0))
    prev_spec = pl.BlockSpec((TIME_BLOCK, d), lambda s: (jnp.clip(s - 1, 0, n_blocks - 1), 0))

    def out_index(s):
        block = jnp.maximum(s - 2, 0)
        return block // steps_per_batch, block % steps_per_batch, 0

    return pl.pallas_call(
        functools.partial(_lru_layer_kernel, steps_per_batch=steps_per_batch, n_blocks=n_blocks),
        grid=(n_blocks + 2,),
        in_specs=[cur_spec, prev_spec, _resident((1, d)), _resident((d, 2 * d)), _resident((CONV_WIDTH, d)),
                  _resident((1, d)), _resident((LRU_BLOCKS, LRU_BLOCK_DIM, 2 * LRU_BLOCK_DIM)),
                  _resident((1, d)), _resident((1, d)), _resident((1, d)),
                  _resident((d, d)), _resident((1, d)), _resident((d, d_ff)), _resident((d_ff, d)),
                  _resident((1, d))],
        out_specs=pl.BlockSpec((None, TIME_BLOCK, d), out_index),
        out_shape=jax.ShapeDtypeStruct((batch, seq, d), _F32),
        scratch_shapes=[pltpu.VMEM((SUBLANES + TIME_BLOCK, d), _F32), pltpu.VMEM((TIME_BLOCK, d), _F32),
                        pltpu.VMEM((TIME_BLOCK, d), _F32), pltpu.VMEM((SUBLANES, d), _F32),
                        pltpu.VMEM((TIME_BLOCK, d), _BF16), pltpu.VMEM((TIME_BLOCK, d), _BF16),
                        pltpu.VMEM((2, TIME_BLOCK, d), _F32)],
        compiler_params=pltpu.CompilerParams(dimension_semantics=("arbitrary",), vmem_limit_bytes=VMEM_LIMIT),
        name="lru_layer",
    )(h, h, g, w_in, conv_w, conv_b, w_gates, b_rg, b_ig, lam, w_out, g_mlp, w_up, w_down, g_final)


def kernel(x, meta_tokens, norm_mix, norm_mlp, sb_w_qkv, sb_w_o, lru_w_in, lru_conv_w, lru_conv_b, lru_w_rg,
           lru_b_rg, lru_w_ig, lru_b_ig, lru_lambda, lru_w_out, mlp_w_up, mlp_w_down, norm_final):
    b, seq, d = x.shape
    assert d == HEADS * HEAD_DIM == LRU_BLOCKS * LRU_BLOCK_DIM
    t_len = N_META + seq
    tp = -(-t_len // TIME_BLOCK) * TIME_BLOCK
    assert (b * tp) % MLP_ROWS == 0 and tp % ATTN_BLOCK == 0

    row = lambda v: v.reshape(1, d)
    q, kt, v, h = _qkv(x, meta_tokens.astype(x.dtype), tp, row(norm_mix[0]), sb_w_qkv[0])
    o = _sb_attention(q, kt, v)
    h = _proj_mlp(o.reshape(b * tp, d), h.reshape(b * tp, d), sb_w_o[0].astype(_BF16), row(norm_mlp[0]),
                  mlp_w_up[0].astype(_BF16), mlp_w_down[0].astype(_BF16), row(norm_final), False)

    w_gates = jnp.concatenate([lru_w_rg[0], lru_w_ig[0]], axis=-1).astype(_BF16)
    return _lru_layer(h, b, seq, row(norm_mix[1]), lru_w_in[0].astype(_BF16), lru_conv_w[0],
                      row(lru_conv_b[0]), w_gates, row(lru_b_rg[0]), row(lru_b_ig[0]), row(lru_lambda[0]),
                      lru_w_out[0].astype(_BF16), row(norm_mlp[1]), mlp_w_up[1].astype(_BF16),
                      mlp_w_down[1].astype(_BF16), row(norm_final))
```

```python
import functools
import math

import jax
import jax.numpy as jnp
from jax import lax
from jax.experimental import pallas as pl
from jax.experimental.pallas import tpu as pltpu

N_META = 16
HEADS = 16
HEAD_DIM = 64
LRU_BLOCKS = 8
LRU_BLOCK_DIM = 128
CONV_WIDTH = 4
LRU_C = 8.0
EPS = 1e-6
LOG2E = 1.4426950408889634

LANES = 128
SUBLANES = 8
TIME_BLOCK = 384
ATTN_BLOCK = 128
ATTN_Q_ROWS = 256
ATTN_KEY_GROUP = 2
ATTN_SUFFIX_BLOCK = 256
ATTN_HEAD_GROUP = 4
ATTN_HEADS_PER_STEP = 8
ATTN_LOG2_WEIGHT_FLOOR = -192.0
MLP_ROWS = 512
MLP_FF_CHUNK = 1024
LRU_SCAN_GROUPS_PER_PIECE = 8
LRU_LAYER_FF_CHUNK = 512
VMEM_LIMIT = 56 * 1024 * 1024

_BF16 = jnp.bfloat16
_F32 = jnp.float32


def _dot(a, b):
    return jnp.dot(a, b, preferred_element_type=_F32)


def _rms_norm(x, g):
    ms = jnp.mean(x * x, axis=-1, keepdims=True)
    return (x * lax.rsqrt(ms + EPS)) * g


def _resident(shape):
    zeros = (0,) * len(shape)
    return pl.BlockSpec(shape, lambda *_: zeros, pipeline_mode=pl.Buffered(1))


def _resident_layer(layer, shape):
    index = (layer,) + (0,) * len(shape)
    return pl.BlockSpec((None,) + shape, lambda *_: index, pipeline_mode=pl.Buffered(1))


def _qkv_kernel(x_ref, meta_ref, g_ref, w_ref, q_ref, kt_ref, v_ref, h_ref, *, pad_rows):
    j, last = pl.program_id(1), pl.num_programs(1) - 1
    rows, d = x_ref.shape
    x = x_ref[...]
    first_block = jnp.concatenate([meta_ref[...], x[:rows - N_META]], axis=0)
    last_block = jnp.concatenate([x[pad_rows:], jnp.zeros((pad_rows, d), x.dtype)], axis=0)
    h = jnp.where(j == 0, first_block, jnp.where(j == last, last_block, x))
    h_ref[...] = h
    hn = _rms_norm(h, g_ref[...]).astype(_BF16)
    q_ref[...] = (_dot(hn, w_ref[:, :d].astype(_BF16)) * (HEAD_DIM ** -0.5 * LOG2E)).astype(_BF16)
    v_ref[...] = _dot(hn, w_ref[:, 2 * d:].astype(_BF16)).astype(_BF16)
    kt = lax.dot_general(w_ref[:, d:2 * d].astype(_BF16), hn, (((0,), (1,)), ((), ())),
                         preferred_element_type=_F32)
    kt_ref[...] = kt.astype(_BF16)


def _qkv(x, meta, tp, g, w_qkv):
    b, seq, d = x.shape
    nt = tp // TIME_BLOCK
    pad_rows = tp - N_META - seq
    assert nt >= 2 and seq >= TIME_BLOCK and all(n % SUBLANES == 0 for n in (pad_rows, N_META, seq, TIME_BLOCK))
    row_spec = pl.BlockSpec((None, TIME_BLOCK, d), lambda i, j: (i, j, 0))
    tiles, meta_tiles = TIME_BLOCK // SUBLANES, N_META // SUBLANES
    x_spec = pl.BlockSpec(
        (pl.Squeezed(), pl.Element(TIME_BLOCK), pl.Element(d)),
        lambda i, j: (i, jnp.clip(j * tiles - meta_tiles, 0, (seq - TIME_BLOCK) // SUBLANES) * SUBLANES, 0))
    return pl.pallas_call(
        functools.partial(_qkv_kernel, pad_rows=pad_rows),
        grid=(b, nt),
        in_specs=[x_spec, _resident((N_META, d)), _resident((1, d)), _resident((d, 3 * d))],
        out_specs=[row_spec, pl.BlockSpec((None, d, TIME_BLOCK), lambda i, j: (i, 0, j)), row_spec, row_spec],
        out_shape=[jax.ShapeDtypeStruct((b, tp, d), _BF16), jax.ShapeDtypeStruct((b, d, tp), _BF16),
                   jax.ShapeDtypeStruct((b, tp, d), _BF16), jax.ShapeDtypeStruct((b, tp, d), _F32)],
        compiler_params=pltpu.CompilerParams(dimension_semantics=("parallel", "parallel"),
                                             vmem_limit_bytes=VMEM_LIMIT),
        name="qkv",
    )(x, meta, g, w_qkv)


def _attn_kernel(q_ref, kt_ref, v_ref, nu_ref, o_ref, acc_ref, carry_ref, z_ref, w_ref):
    blk, qrows, group = ATTN_BLOCK, ATTN_Q_ROWS, ATTN_KEY_GROUP
    n_heads = q_ref.shape[1] // HEAD_DIM
    lanes = ATTN_HEAD_GROUP * HEAD_DIM

    def group_lanes(hd):
        lo = hd // ATTN_HEAD_GROUP * lanes
        return slice(lo, lo + lanes)

    def head_of_lane(rows):
        return lax.broadcasted_iota(jnp.int32, (rows, lanes), 1) // HEAD_DIM

    def stacked_values(key_blk, keys, first_head):
        v = v_ref[pl.ds(pl.multiple_of(key_blk * blk, blk), keys), group_lanes(first_head)]
        v_head = head_of_lane(keys)
        return jnp.concatenate([jnp.where(v_head == hd, v, jnp.zeros_like(v)) for hd in range(ATTN_HEAD_GROUP)],
                               axis=0)

    def scores(q_head, hd, key_blk, keys):
        return _dot(q_head, kt_ref[group_lanes(hd), pl.ds(pl.multiple_of(key_blk * blk, blk), keys)])

    def visibility(rows, n_blk, causal_shift):
        masks = []
        for k_idx in range(n_blk):
            if causal_shift is None or (k_idx + 1) * blk - causal_shift <= 0:
                masks.append(None)
            else:
                key_pos = lax.broadcasted_iota(jnp.int32, (rows, blk), 1) + (k_idx * blk - causal_shift)
                masks.append(key_pos < lax.broadcasted_iota(jnp.int32, (rows, blk), 0))
        return masks

    def softplus_keys(z, masks):
        sp = jnp.maximum(jnp.log2(1.0 + jnp.exp2(jnp.minimum(z, 100.0))), z)
        sp_keys = jnp.concatenate([sp[:, k * blk:(k + 1) * blk] if m is None else
                                   jnp.where(m, sp[:, k * blk:(k + 1) * blk], 0.0)
                                   for k, m in enumerate(masks)], axis=1).astype(_BF16)
        return z - sp, sp_keys

    def suffix_sums(sp, widths):
        out, lo = [], 0
        for width in widths:
            out.append(_dot(sp[:, lo:lo + width], nu_ref[:width, :width]))
            lo += width
        return out

    def weights_from(log_sig, sp, suffixes, hd, rows, widths, masks):
        carry = carry_ref[hd, :rows, :]
        w_head = [None] * len(masks)
        lo = sum(widths)
        for width, suffix in zip(reversed(widths), reversed(suffixes)):
            lo -= width
            for c in range(width // blk):
                k_idx = lo // blk + c
                w = jnp.exp2(log_sig[:, k_idx * blk:(k_idx + 1) * blk] + suffix[:, c * blk:(c + 1) * blk] + carry)
                if masks[k_idx] is not None:
                    w = jnp.where(masks[k_idx], w, 0.0)
                w_head[k_idx] = w.astype(_BF16)
            total = suffix[:, :blk] - sp[:, lo:lo + blk].astype(_F32)
            carry = carry + jnp.broadcast_to(total[:, 0:1], (rows, LANES))
        carry_ref[hd, :rows, :] = carry
        return w_head

    def sweep(qh, rows, key_blk, widths, causal_shift):
        keys = sum(widths)
        masks = visibility(rows, keys // blk, causal_shift)
        v = v_ref[pl.ds(pl.multiple_of(key_blk * blk, blk), keys), :]
        v_head = head_of_lane(keys)
        zs, terms, suffixes = {}, {}, {}
        for tick in range(n_heads + 2):
            if tick < n_heads:
                zs[tick] = scores(qh[tick], tick, key_blk, keys)
            if 0 <= tick - 1 < n_heads:
                terms[tick - 1] = softplus_keys(zs.pop(tick - 1), masks)
                suffixes[tick - 1] = suffix_sums(terms[tick - 1][1], widths)
            if 0 <= tick - 2 < n_heads:
                hd = tick - 2
                w = jnp.concatenate(weights_from(*terms.pop(hd), suffixes.pop(hd), hd, rows, widths, masks), axis=1)
                v_group = v[:, group_lanes(hd)]
                acc_ref[:rows, group_lanes(hd)] += _dot(
                    w, jnp.where(v_head == hd % ATTN_HEAD_GROUP, v_group, jnp.zeros_like(v_group)))

    def start_block(row0, rows):
        q = q_ref[pl.ds(row0, rows), :]
        q_head = head_of_lane(rows)
        acc_ref[...] = jnp.zeros_like(acc_ref)
        carry_ref[...] = jnp.zeros_like(carry_ref)
        groups = [q[:, group_lanes(hd)] for hd in range(0, n_heads, ATTN_HEAD_GROUP)]
        return [jnp.where(q_head == hd % ATTN_HEAD_GROUP, groups[hd // ATTN_HEAD_GROUP],
                          jnp.zeros_like(groups[0])) for hd in range(n_heads)]

    def write(row0, rows):
        o_ref[pl.ds(row0, rows), :] = acc_ref[:rows, :].astype(o_ref.dtype)

    def weights_live():
        return (jnp.max(carry_ref[...]) > ATTN_LOG2_WEIGHT_FLOOR).astype(jnp.int32)

    wide = ATTN_SUFFIX_BLOCK
    group_widths = [wide] * (group * blk // wide)
    group_keys = group * blk
    near = wide // blk
    tail_blocks = 1 + group - near
    tail_widths = [blk] * (tail_blocks % near) + [wide] * (tail_blocks // near)
    assert group_keys == qrows and 0 < near <= group

    sweep(start_block(0, blk), blk, 0, [blk], 0)
    write(0, blk)
    sweep(start_block(blk, qrows), qrows, 0, [blk] + group_widths, blk)
    write(blk, qrows)

    def wide_block(qi, _):
        row0 = pl.multiple_of(blk + qi * qrows, blk)
        first = 1 + qi * group
        qh = start_block(row0, qrows)
        sweep(qh, qrows, first - near, [wide] + group_widths, wide)

        @pl.when(weights_live() > 0)
        def _():
            def item_start(g):
                return jnp.maximum(first - near - (g + 1) * group, 0)

            def pv(g):
                for hd in range(0, n_heads, ATTN_HEAD_GROUP):
                    cols = slice(hd * group_keys, (hd + ATTN_HEAD_GROUP) * group_keys)
                    acc_ref[:, group_lanes(hd)] += _dot(w_ref[:, cols],
                                                        stacked_values(item_start(g), group_keys, hd))

            for hd in range(n_heads):
                z_ref[hd] = scores(qh[hd], hd, item_start(0), group_keys)
            w_ref[...] = jnp.zeros_like(w_ref)

            def key_group(state):
                g, _ = state
                pv(g - 1)
                no_mask = [None] * group

                def prepare(hd):
                    log_sig, sp = softplus_keys(z_ref[hd], no_mask)
                    z_ref[hd] = log_sig
                    return sp, suffix_sums(sp, group_widths)

                ahead = prepare(0)
                for hd in range(n_heads):
                    sp, suffixes = ahead
                    if hd + 1 < n_heads:
                        ahead = prepare(hd + 1)
                    w_head = weights_from(z_ref[hd], sp, suffixes, hd, qrows, group_widths, no_mask)
                    w_ref[:, hd * group_keys:(hd + 1) * group_keys] = jnp.concatenate(w_head, axis=1)
                    z_ref[hd] = scores(qh[hd], hd, item_start(g + 1), group_keys)
                return g + 1, weights_live()

            n_done, live = lax.while_loop(lambda state: jnp.logical_and(state[0] < qi - 1, state[1] > 0),
                                          key_group, (jnp.int32(0), jnp.int32(1)))
            pv(n_done - 1)

            @pl.when(live > 0)
            def _():
                sweep(qh, qrows, 0, tail_widths, None)

        write(row0, qrows)
        return 0

    lax.fori_loop(1, (q_ref.shape[0] - blk) // qrows, wide_block, 0)


def _suffix_matrix():
    j = jnp.arange(ATTN_SUFFIX_BLOCK)[:, None]
    s = jnp.arange(ATTN_SUFFIX_BLOCK)[None, :]
    return -(j > s).astype(_BF16)


def _sb_attention(q, kt, v):
    b, tp, d = q.shape
    assert (tp - ATTN_BLOCK) % ATTN_Q_ROWS == 0 and (ATTN_Q_ROWS // ATTN_BLOCK) % ATTN_KEY_GROUP == 0
    lanes = ATTN_HEADS_PER_STEP * HEAD_DIM
    return pl.pallas_call(
        _attn_kernel,
        grid=(b, d // lanes),
        in_specs=[pl.BlockSpec((None, tp, lanes), lambda i, j: (i, 0, j)),
                  pl.BlockSpec((None, lanes, tp), lambda i, j: (i, j, 0)),
                  pl.BlockSpec((None, tp, lanes), lambda i, j: (i, 0, j)),
                  _resident((ATTN_SUFFIX_BLOCK, ATTN_SUFFIX_BLOCK))],
        out_specs=pl.BlockSpec((None, tp, lanes), lambda i, j: (i, 0, j)),
        out_shape=jax.ShapeDtypeStruct((b, tp, d), _BF16),
        scratch_shapes=[pltpu.VMEM((ATTN_Q_ROWS, lanes), _F32),
                        pltpu.VMEM((ATTN_HEADS_PER_STEP, ATTN_Q_ROWS, LANES), _F32),
                        pltpu.VMEM((ATTN_HEADS_PER_STEP, ATTN_Q_ROWS, ATTN_KEY_GROUP * ATTN_BLOCK), _F32),
                        pltpu.VMEM((ATTN_Q_ROWS, ATTN_HEADS_PER_STEP * ATTN_KEY_GROUP * ATTN_BLOCK), _BF16)],
        compiler_params=pltpu.CompilerParams(dimension_semantics=("parallel", "parallel"),
                                             vmem_limit_bytes=VMEM_LIMIT),
        name="sb_attn",
    )(q, kt, v, _suffix_matrix())


def _gelu_tanh(x):
    c = math.sqrt(2.0 / math.pi)
    half = 0.5 * x
    return half + half * jnp.tanh(x * ((0.044715 * c) * (x * x) + c))


def _sigmoid(x):
    return 0.5 + 0.5 * jnp.tanh(0.5 * x)


def _lru_stage(h_ref, g_ref, win_ref, cw_ref, cb_ref, wg_ref, brg_ref, big_ref, lam_ref, y_ref,
               rec_ref, a_ref, b_ref, state_ref):
    rows, d = h_ref.shape
    tail = SUBLANES

    hn = _rms_norm(h_ref[...], g_ref[...]).astype(_BF16)
    gate = _gelu_tanh(_dot(hn, win_ref[:, :d]))
    rec_ref[tail:, :] = _dot(hn, win_ref[:, d:])
    yield

    u = cb_ref[...] + cw_ref[CONV_WIDTH - 1:CONV_WIDTH, :] * rec_ref[tail:, :]
    for j in range(CONV_WIDTH - 1):
        shift = CONV_WIDTH - 1 - j
        u = u + cw_ref[j:j + 1, :] * rec_ref[tail - shift:tail - shift + rows, :]
    rec_ref[0:tail, :] = rec_ref[rows:rows + tail, :]
    yield

    neg_lam = -lam_ref[...]
    log_a_unit = -LRU_C * (jnp.maximum(neg_lam, 0.0) + jnp.log1p(jnp.exp(-jnp.abs(neg_lam))))
    ub = u.astype(_BF16)
    for n in range(LRU_BLOCKS):
        cols = slice(n * LRU_BLOCK_DIM, (n + 1) * LRU_BLOCK_DIM)
        ri = _dot(ub[:, cols], wg_ref[n])
        r = _sigmoid(ri[:, :LRU_BLOCK_DIM] + brg_ref[:, cols])
        i = _sigmoid(ri[:, LRU_BLOCK_DIM:] + big_ref[:, cols])
        log_a = log_a_unit[:, cols] * r
        a = jnp.exp(log_a)
        one_minus_a2 = -jnp.tanh(log_a) * (1.0 + a * a)
        mult = jnp.where(one_minus_a2 > 0.0, one_minus_a2 * lax.rsqrt(one_minus_a2), 0.0)
        a_ref[:, cols] = a
        b_ref[:, cols] = mult * (i * u[:, cols])
        yield

    sub = lax.broadcasted_iota(jnp.int32, (SUBLANES, d), 0)
    state = state_ref[...]
    for gidx in range(rows // SUBLANES):
        r0 = gidx * SUBLANES
        av = a_ref[r0:r0 + SUBLANES, :]
        bv = b_ref[r0:r0 + SUBLANES, :]
        for s in (1, 2, 4):
            keep = sub >= s
            bv = jnp.where(keep, av * pltpu.roll(bv, s, axis=0) + bv, bv)
            av = jnp.where(keep, av * pltpu.roll(av, s, axis=0), av)
        hv = av * state + bv
        b_ref[r0:r0 + SUBLANES, :] = hv
        state = jnp.broadcast_to(hv[SUBLANES - 1:SUBLANES, :], (SUBLANES, d))
        if gidx % LRU_SCAN_GROUPS_PER_PIECE == LRU_SCAN_GROUPS_PER_PIECE - 1:
            yield
    state_ref[...] = state
    y_ref[...] = (b_ref[...] * gate).astype(y_ref.dtype)


def _mlp_stage(y, h, wo_ref, g_ref, wup_ref, wdn_ref, gf_ref, o_ref, hn_ref, final_norm, ff_chunk):
    d_ff = wup_ref.shape[1]
    h1 = h + _dot(y, wo_ref[...])
    hn_ref[...] = _rms_norm(h1, g_ref[...]).astype(_BF16)
    o_ref[...] = h1
    yield
    for c in range(0, d_ff, ff_chunk):
        up = jnp.maximum(_dot(hn_ref[...], wup_ref[:, c:c + ff_chunk]), 0.0)
        up = (up * up).astype(_BF16)
        yield
        o_ref[...] += _dot(up, wdn_ref[c:c + ff_chunk, :])
        yield
    if final_norm:
        o_ref[...] = _rms_norm(o_ref[...], gf_ref[...])


def _interleave(*stages):
    live = list(stages)
    while live:
        for stage in list(live):
            if next(stage, StopIteration) is StopIteration:
                live.remove(stage)


def _mlp_kernel(y_ref, h_ref, wo_ref, g_ref, wup_ref, wdn_ref, gf_ref, o_ref, hn_ref, *, final_norm):
    _interleave(_mlp_stage(y_ref[...], h_ref[...], wo_ref, g_ref, wup_ref, wdn_ref, gf_ref, o_ref, hn_ref,
                           final_norm, MLP_FF_CHUNK))


def _proj_mlp(y, h, w_o, g, layer, w_up, w_down, g_final, final_norm):
    m, d = h.shape
    d_ff = w_up.shape[2]
    row_spec = pl.BlockSpec((MLP_ROWS, d), lambda i: (i, 0))
    return pl.pallas_call(
        functools.partial(_mlp_kernel, final_norm=final_norm),
        grid=(m // MLP_ROWS,),
        in_specs=[row_spec, row_spec, _resident((d, d)), _resident((1, d)), _resident_layer(layer, (d, d_ff)),
                  _resident_layer(layer, (d_ff, d)), _resident((1, d))],
        out_specs=row_spec,
        out_shape=jax.ShapeDtypeStruct((m, d), _F32),
        scratch_shapes=[pltpu.VMEM((MLP_ROWS, d), _BF16)],
        compiler_params=pltpu.CompilerParams(dimension_semantics=("parallel",), vmem_limit_bytes=VMEM_LIMIT),
        name="proj_mlp",
    )(y, h, w_o, g, w_up, w_down, g_final)


def _lru_layer_kernel(hcur_ref, hprev_ref, g_ref, win_ref, cw_ref, cb_ref, wg_ref, brg_ref, big_ref, lam_ref,
                      wo_ref, gm_ref, wup_ref, wdn_ref, gf_ref, out_ref,
                      rec_ref, a_ref, b_ref, state_ref, y_ref, hn_ref, obuf_ref, *, steps_per_batch, n_blocks):
    s = pl.program_id(0)
    rows = hcur_ref.shape[0]
    cur, prev = s % 2, 1 - s % 2

    @pl.when(s == 0)
    def _():
        y_ref[...] = jnp.zeros_like(y_ref)
        obuf_ref[...] = jnp.zeros_like(obuf_ref)

    @pl.when(s % steps_per_batch == 0)
    def _():
        rec_ref[0:SUBLANES, :] = jnp.zeros((SUBLANES, rec_ref.shape[1]), _F32)
        state_ref[...] = jnp.zeros_like(state_ref)

    @pl.when(s <= n_blocks)
    def _():
        y_prev = y_ref[...]
        _interleave(
            _lru_stage(hcur_ref, g_ref, win_ref, cw_ref, cb_ref, wg_ref, brg_ref, big_ref, lam_ref, y_ref,
                       rec_ref, a_ref, b_ref, state_ref),
            _mlp_stage(y_prev, hprev_ref[...], wo_ref, gm_ref, wup_ref, wdn_ref, gf_ref,
                       obuf_ref.at[cur], hn_ref, True, LRU_LAYER_FF_CHUNK))

    out_ref[0:rows - N_META, :] = obuf_ref[prev, N_META:rows, :]
    out_ref[rows - N_META:rows, :] = obuf_ref[cur, 0:N_META, :]


def _lru_layer(h, batch, seq, g, w_in, conv_w, conv_b, w_gates, b_rg, b_ig, lam, w_out, g_mlp, layer, w_up,
               w_down, g_final):
    m, d = h.shape
    d_ff = w_up.shape[2]
    n_blocks = m // TIME_BLOCK
    steps_per_batch = n_blocks // batch
    cur_spec = pl.BlockSpec((TIME_BLOCK, d), lambda s: (jnp.minimum(s, n_blocks - 1), 0))
    prev_spec = pl.BlockSpec((TIME_BLOCK, d), lambda s: (jnp.clip(s - 1, 0, n_blocks - 1), 0))

    def out_index(s):
        block = jnp.maximum(s - 2, 0)
        return block // steps_per_batch, block % steps_per_batch, 0

    return pl.pallas_call(
        functools.partial(_lru_layer_kernel, steps_per_batch=steps_per_batch, n_blocks=n_blocks),
        grid=(n_blocks + 2,),
        in_specs=[cur_spec, prev_spec, _resident((1, d)), _resident((d, 2 * d)), _resident((CONV_WIDTH, d)),
                  _resident((1, d)), _resident((LRU_BLOCKS, LRU_BLOCK_DIM, 2 * LRU_BLOCK_DIM)),
                  _resident((1, d)), _resident((1, d)), _resident((1, d)),
                  _resident((d, d)), _resident((1, d)), _resident_layer(layer, (d, d_ff)),
                  _resident_layer(layer, (d_ff, d)), _resident((1, d))],
        out_specs=pl.BlockSpec((None, TIME_BLOCK, d), out_index),
        out_shape=jax.ShapeDtypeStruct((batch, seq, d), _F32),
        scratch_shapes=[pltpu.VMEM((SUBLANES + TIME_BLOCK, d), _F32), pltpu.VMEM((TIME_BLOCK, d), _F32),
                        pltpu.VMEM((TIME_BLOCK, d), _F32), pltpu.VMEM((SUBLANES, d), _F32),
                        pltpu.VMEM((TIME_BLOCK, d), _BF16), pltpu.VMEM((TIME_BLOCK, d), _BF16),
                        pltpu.VMEM((2, TIME_BLOCK, d), _F32)],
        compiler_params=pltpu.CompilerParams(dimension_semantics=("arbitrary",), vmem_limit_bytes=VMEM_LIMIT),
        name="lru_layer",
    )(h, h, g, w_in, conv_w, conv_b, w_gates, b_rg, b_ig, lam, w_out, g_mlp, w_up, w_down, g_final)


def kernel(x, meta_tokens, norm_mix, norm_mlp, sb_w_qkv, sb_w_o, lru_w_in, lru_conv_w, lru_conv_b, lru_w_rg,
           lru_b_rg, lru_w_ig, lru_b_ig, lru_lambda, lru_w_out, mlp_w_up, mlp_w_down, norm_final):
    b, seq, d = x.shape
    assert d == HEADS * HEAD_DIM == LRU_BLOCKS * LRU_BLOCK_DIM
    t_len = N_META + seq
    tp = -(-t_len // TIME_BLOCK) * TIME_BLOCK
    assert (b * tp) % MLP_ROWS == 0 and tp % ATTN_BLOCK == 0

    row = lambda v: v.reshape(1, d)
    q, kt, v, h = _qkv(x, meta_tokens.astype(x.dtype), tp, row(norm_mix[0]), sb_w_qkv[0])
    o = _sb_attention(q, kt, v)
    w_up, w_down = mlp_w_up.astype(_BF16), mlp_w_down.astype(_BF16)
    h = _proj_mlp(o.reshape(b * tp, d), h.reshape(b * tp, d), sb_w_o[0].astype(_BF16), row(norm_mlp[0]),
                  0, w_up, w_down, row(norm_final), False)

    w_gates = jnp.concatenate([lru_w_rg[0], lru_w_ig[0]], axis=-1).astype(_BF16)
    return _lru_layer(h, b, seq, row(norm_mix[1]), lru_w_in[0].astype(_BF16), lru_conv_w[0],
                      row(lru_conv_b[0]), w_gates, row(lru_b_rg[0]), row(lru_b_ig[0]), row(lru_lambda[0]),
                      lru_w_out[0].astype(_BF16), row(norm_mlp[1]), 1, w_up, w_down, row(norm_final))
```

```python
import functools
import math

import jax
import jax.numpy as jnp
from jax import lax
from jax.experimental import pallas as pl
from jax.experimental.pallas import tpu as pltpu

N_META = 16
HEADS = 16
HEAD_DIM = 64
LRU_BLOCKS = 8
LRU_BLOCK_DIM = 128
CONV_WIDTH = 4
LRU_C = 8.0
EPS = 1e-6
LOG2E = 1.4426950408889634

LANES = 128
SUBLANES = 8
TIME_BLOCK = 384
ATTN_BLOCK = 128
ATTN_Q_ROWS = 256
ATTN_KEY_GROUP = 2
ATTN_SUFFIX_BLOCK = 256
ATTN_HEAD_GROUP = 4
ATTN_HEADS_PER_STEP = 8
ATTN_LOG2_WEIGHT_FLOOR = -192.0
MLP_ROWS = 512
MLP_FF_CHUNK = 1024
LRU_SCAN_GROUPS_PER_PIECE = 8
LRU_LAYER_FF_CHUNK = 512
VMEM_LIMIT = 56 * 1024 * 1024

_BF16 = jnp.bfloat16
_F32 = jnp.float32


def _dot(a, b):
    return jnp.dot(a, b, preferred_element_type=_F32)


def _rms_norm(x, g):
    ms = jnp.mean(x * x, axis=-1, keepdims=True)
    return (x * lax.rsqrt(ms + EPS)) * g


def _resident(shape):
    zeros = (0,) * len(shape)
    return pl.BlockSpec(shape, lambda *_: zeros, pipeline_mode=pl.Buffered(1))


def _qkv_kernel(x_ref, meta_ref, g_ref, w_ref, q_ref, kt_ref, v_ref, h_ref, *, pad_rows):
    j, last = pl.program_id(1), pl.num_programs(1) - 1
    rows, d = x_ref.shape
    x = x_ref[...]
    first_block = jnp.concatenate([meta_ref[...], x[:rows - N_META]], axis=0)
    last_block = jnp.concatenate([x[pad_rows:], jnp.zeros((pad_rows, d), x.dtype)], axis=0)
    h = jnp.where(j == 0, first_block, jnp.where(j == last, last_block, x))
    h_ref[...] = h
    hn = _rms_norm(h, g_ref[...]).astype(_BF16)
    q_ref[...] = (_dot(hn, w_ref[:, :d].astype(_BF16)) * (HEAD_DIM ** -0.5 * LOG2E)).astype(_BF16)
    v_ref[...] = _dot(hn, w_ref[:, 2 * d:].astype(_BF16)).astype(_BF16)
    kt = lax.dot_general(w_ref[:, d:2 * d].astype(_BF16), hn, (((0,), (1,)), ((), ())),
                         preferred_element_type=_F32)
    kt_ref[...] = kt.astype(_BF16)


def _qkv(x, meta, tp, g, w_qkv):
    b, seq, d = x.shape
    nt = tp // TIME_BLOCK
    pad_rows = tp - N_META - seq
    assert nt >= 2 and seq >= TIME_BLOCK and all(n % SUBLANES == 0 for n in (pad_rows, N_META, seq, TIME_BLOCK))
    row_spec = pl.BlockSpec((None, TIME_BLOCK, d), lambda i, j: (i, j, 0))
    tiles, meta_tiles = TIME_BLOCK // SUBLANES, N_META // SUBLANES
    x_spec = pl.BlockSpec(
        (pl.Squeezed(), pl.Element(TIME_BLOCK), pl.Element(d)),
        lambda i, j: (i, jnp.clip(j * tiles - meta_tiles, 0, (seq - TIME_BLOCK) // SUBLANES) * SUBLANES, 0))
    return pl.pallas_call(
        functools.partial(_qkv_kernel, pad_rows=pad_rows),
        grid=(b, nt),
        in_specs=[x_spec, _resident((N_META, d)), _resident((1, d)), _resident((d, 3 * d))],
        out_specs=[row_spec, pl.BlockSpec((None, d, TIME_BLOCK), lambda i, j: (i, 0, j)), row_spec, row_spec],
        out_shape=[jax.ShapeDtypeStruct((b, tp, d), _BF16), jax.ShapeDtypeStruct((b, d, tp), _BF16),
                   jax.ShapeDtypeStruct((b, tp, d), _BF16), jax.ShapeDtypeStruct((b, tp, d), _F32)],
        compiler_params=pltpu.CompilerParams(dimension_semantics=("parallel", "parallel"),
                                             vmem_limit_bytes=VMEM_LIMIT),
        name="qkv",
    )(x, meta, g, w_qkv)


def _attn_kernel(q_ref, kt_ref, v_ref, nu_ref, o_ref, acc_ref, carry_ref, z_ref, w_ref):
    blk, qrows, group = ATTN_BLOCK, ATTN_Q_ROWS, ATTN_KEY_GROUP
    n_heads = q_ref.shape[1] // HEAD_DIM
    lanes = ATTN_HEAD_GROUP * HEAD_DIM

    def group_lanes(hd):
        lo = hd // ATTN_HEAD_GROUP * lanes
        return slice(lo, lo + lanes)

    def head_of_lane(rows):
        return lax.broadcasted_iota(jnp.int32, (rows, lanes), 1) // HEAD_DIM

    def stacked_values(key_blk, keys, first_head):
        v = v_ref[pl.ds(pl.multiple_of(key_blk * blk, blk), keys), group_lanes(first_head)]
        v_head = head_of_lane(keys)
        return jnp.concatenate([jnp.where(v_head == hd, v, jnp.zeros_like(v)) for hd in range(ATTN_HEAD_GROUP)],
                               axis=0)

    def scores(q_head, hd, key_blk, keys):
        return _dot(q_head, kt_ref[group_lanes(hd), pl.ds(pl.multiple_of(key_blk * blk, blk), keys)])

    def visibility(rows, n_blk, causal_shift):
        masks = []
        for k_idx in range(n_blk):
            if causal_shift is None or (k_idx + 1) * blk - causal_shift <= 0:
                masks.append(None)
            else:
                key_pos = lax.broadcasted_iota(jnp.int32, (rows, blk), 1) + (k_idx * blk - causal_shift)
                masks.append(key_pos < lax.broadcasted_iota(jnp.int32, (rows, blk), 0))
        return masks

    def softplus_keys(z, masks):
        sp = jnp.maximum(jnp.log2(1.0 + jnp.exp2(jnp.minimum(z, 100.0))), z)
        sp_keys = jnp.concatenate([sp[:, k * blk:(k + 1) * blk] if m is None else
                                   jnp.where(m, sp[:, k * blk:(k + 1) * blk], 0.0)
                                   for k, m in enumerate(masks)], axis=1).astype(_BF16)
        return z - sp, sp_keys

    def suffix_sums(sp, widths):
        out, lo = [], 0
        for width in widths:
            out.append(_dot(sp[:, lo:lo + width], nu_ref[:width, :width]))
            lo += width
        return out

    def weights_from(log_sig, sp, suffixes, hd, rows, widths, masks):
        carry = carry_ref[hd, :rows, :]
        w_head = [None] * len(masks)
        lo = sum(widths)
        for width, suffix in zip(reversed(widths), reversed(suffixes)):
            lo -= width
            for c in range(width // blk):
                k_idx = lo // blk + c
                w = jnp.exp2(log_sig[:, k_idx * blk:(k_idx + 1) * blk] + suffix[:, c * blk:(c + 1) * blk] + carry)
                if masks[k_idx] is not None:
                    w = jnp.where(masks[k_idx], w, 0.0)
                w_head[k_idx] = w.astype(_BF16)
            total = suffix[:, :blk] - sp[:, lo:lo + blk].astype(_F32)
            carry = carry + jnp.broadcast_to(total[:, 0:1], (rows, LANES))
        carry_ref[hd, :rows, :] = carry
        return w_head

    def sweep(qh, rows, key_blk, widths, causal_shift):
        keys = sum(widths)
        masks = visibility(rows, keys // blk, causal_shift)
        v = v_ref[pl.ds(pl.multiple_of(key_blk * blk, blk), keys), :]
        v_head = head_of_lane(keys)
        zs, terms, suffixes = {}, {}, {}
        for tick in range(n_heads + 2):
            if tick < n_heads:
                zs[tick] = scores(qh[tick], tick, key_blk, keys)
            if 0 <= tick - 1 < n_heads:
                terms[tick - 1] = softplus_keys(zs.pop(tick - 1), masks)
                suffixes[tick - 1] = suffix_sums(terms[tick - 1][1], widths)
            if 0 <= tick - 2 < n_heads:
                hd = tick - 2
                w = jnp.concatenate(weights_from(*terms.pop(hd), suffixes.pop(hd), hd, rows, widths, masks), axis=1)
                v_group = v[:, group_lanes(hd)]
                acc_ref[:rows, group_lanes(hd)] += _dot(
                    w, jnp.where(v_head == hd % ATTN_HEAD_GROUP, v_group, jnp.zeros_like(v_group)))

    def start_block(row0, rows):
        q = q_ref[pl.ds(row0, rows), :]
        q_head = head_of_lane(rows)
        acc_ref[...] = jnp.zeros_like(acc_ref)
        carry_ref[...] = jnp.zeros_like(carry_ref)
        groups = [q[:, group_lanes(hd)] for hd in range(0, n_heads, ATTN_HEAD_GROUP)]
        return [jnp.where(q_head == hd % ATTN_HEAD_GROUP, groups[hd // ATTN_HEAD_GROUP],
                          jnp.zeros_like(groups[0])) for hd in range(n_heads)]

    def write(row0, rows):
        o_ref[pl.ds(row0, rows), :] = acc_ref[:rows, :].astype(o_ref.dtype)

    def weights_live():
        return (jnp.max(carry_ref[...]) > ATTN_LOG2_WEIGHT_FLOOR).astype(jnp.int32)

    wide = ATTN_SUFFIX_BLOCK
    group_widths = [wide] * (group * blk // wide)
    group_keys = group * blk
    near = wide // blk
    tail_blocks = 1 + group - near
    tail_widths = [blk] * (tail_blocks % near) + [wide] * (tail_blocks // near)
    assert group_keys == qrows and 0 < near <= group

    sweep(start_block(0, blk), blk, 0, [blk], 0)
    write(0, blk)
    sweep(start_block(blk, qrows), qrows, 0, [blk] + group_widths, blk)
    write(blk, qrows)

    def wide_block(qi, _):
        row0 = pl.multiple_of(blk + qi * qrows, blk)
        first = 1 + qi * group
        qh = start_block(row0, qrows)
        sweep(qh, qrows, first - near, [wide] + group_widths, wide)

        @pl.when(weights_live() > 0)
        def _():
            def item_start(g):
                return jnp.maximum(first - near - (g + 1) * group, 0)

            def pv(g):
                for hd in range(0, n_heads, ATTN_HEAD_GROUP):
                    cols = slice(hd * group_keys, (hd + ATTN_HEAD_GROUP) * group_keys)
                    acc_ref[:, group_lanes(hd)] += _dot(w_ref[:, cols],
                                                        stacked_values(item_start(g), group_keys, hd))

            for hd in range(n_heads):
                z_ref[hd] = scores(qh[hd], hd, item_start(0), group_keys)
            w_ref[...] = jnp.zeros_like(w_ref)

            def key_group(state):
                g, _ = state
                pv(g - 1)
                no_mask = [None] * group

                def prepare(hd):
                    log_sig, sp = softplus_keys(z_ref[hd], no_mask)
                    z_ref[hd] = log_sig
                    return sp, suffix_sums(sp, group_widths)

                ahead = prepare(0)
                for hd in range(n_heads):
                    sp, suffixes = ahead
                    if hd + 1 < n_heads:
                        ahead = prepare(hd + 1)
                    w_head = weights_from(z_ref[hd], sp, suffixes, hd, qrows, group_widths, no_mask)
                    w_ref[:, hd * group_keys:(hd + 1) * group_keys] = jnp.concatenate(w_head, axis=1)
                    z_ref[hd] = scores(qh[hd], hd, item_start(g + 1), group_keys)
                return g + 1, weights_live()

            n_done, live = lax.while_loop(lambda state: jnp.logical_and(state[0] < qi - 1, state[1] > 0),
                                          key_group, (jnp.int32(0), jnp.int32(1)))
            pv(n_done - 1)

            @pl.when(live > 0)
            def _():
                sweep(qh, qrows, 0, tail_widths, None)

        write(row0, qrows)
        return 0

    lax.fori_loop(1, (q_ref.shape[0] - blk) // qrows, wide_block, 0)


def _suffix_matrix():
    j = jnp.arange(ATTN_SUFFIX_BLOCK)[:, None]
    s = jnp.arange(ATTN_SUFFIX_BLOCK)[None, :]
    return -(j > s).astype(_BF16)


def _sb_attention(q, kt, v):
    b, tp, d = q.shape
    assert (tp - ATTN_BLOCK) % ATTN_Q_ROWS == 0 and (ATTN_Q_ROWS // ATTN_BLOCK) % ATTN_KEY_GROUP == 0
    lanes = ATTN_HEADS_PER_STEP * HEAD_DIM
    return pl.pallas_call(
        _attn_kernel,
        grid=(b, d // lanes),
        in_specs=[pl.BlockSpec((None, tp, lanes), lambda i, j: (i, 0, j)),
                  pl.BlockSpec((None, lanes, tp), lambda i, j: (i, j, 0)),
                  pl.BlockSpec((None, tp, lanes), lambda i, j: (i, 0, j)),
                  _resident((ATTN_SUFFIX_BLOCK, ATTN_SUFFIX_BLOCK))],
        out_specs=pl.BlockSpec((None, tp, lanes), lambda i, j: (i, 0, j)),
        out_shape=jax.ShapeDtypeStruct((b, tp, d), _BF16),
        scratch_shapes=[pltpu.VMEM((ATTN_Q_ROWS, lanes), _F32),
                        pltpu.VMEM((ATTN_HEADS_PER_STEP, ATTN_Q_ROWS, LANES), _F32),
                        pltpu.VMEM((ATTN_HEADS_PER_STEP, ATTN_Q_ROWS, ATTN_KEY_GROUP * ATTN_BLOCK), _F32),
                        pltpu.VMEM((ATTN_Q_ROWS, ATTN_HEADS_PER_STEP * ATTN_KEY_GROUP * ATTN_BLOCK), _BF16)],
        compiler_params=pltpu.CompilerParams(dimension_semantics=("parallel", "parallel"),
                                             vmem_limit_bytes=VMEM_LIMIT),
        name="sb_attn",
    )(q, kt, v, _suffix_matrix())


def _gelu_tanh(x):
    c = math.sqrt(2.0 / math.pi)
    half = 0.5 * x
    return half + half * jnp.tanh(x * ((0.044715 * c) * (x * x) + c))


def _sigmoid(x):
    return 0.5 + 0.5 * jnp.tanh(0.5 * x)


def _lru_stage(h_ref, g_ref, win_ref, cw_ref, cb_ref, wg_ref, brg_ref, big_ref, lam_ref, y_ref,
               rec_ref, a_ref, b_ref, state_ref):
    rows, d = h_ref.shape
    tail = SUBLANES

    hn = _rms_norm(h_ref[...], g_ref[...]).astype(_BF16)
    gate = _gelu_tanh(_dot(hn, win_ref[:, :d]))
    rec_ref[tail:, :] = _dot(hn, win_ref[:, d:])
    yield

    u = cb_ref[...] + cw_ref[CONV_WIDTH - 1:CONV_WIDTH, :] * rec_ref[tail:, :]
    for j in range(CONV_WIDTH - 1):
        shift = CONV_WIDTH - 1 - j
        u = u + cw_ref[j:j + 1, :] * rec_ref[tail - shift:tail - shift + rows, :]
    rec_ref[0:tail, :] = rec_ref[rows:rows + tail, :]
    yield

    neg_lam = -lam_ref[...]
    log_a_unit = -LRU_C * (jnp.maximum(neg_lam, 0.0) + jnp.log1p(jnp.exp(-jnp.abs(neg_lam))))
    ub = u.astype(_BF16)
    for n in range(LRU_BLOCKS):
        cols = slice(n * LRU_BLOCK_DIM, (n + 1) * LRU_BLOCK_DIM)
        ri = _dot(ub[:, cols], wg_ref[n])
        r = _sigmoid(ri[:, :LRU_BLOCK_DIM] + brg_ref[:, cols])
        i = _sigmoid(ri[:, LRU_BLOCK_DIM:] + big_ref[:, cols])
        log_a = log_a_unit[:, cols] * r
        a = jnp.exp(log_a)
        one_minus_a2 = -jnp.tanh(log_a) * (1.0 + a * a)
        mult = jnp.where(one_minus_a2 > 0.0, one_minus_a2 * lax.rsqrt(one_minus_a2), 0.0)
        a_ref[:, cols] = a
        b_ref[:, cols] = mult * (i * u[:, cols])
        yield

    sub = lax.broadcasted_iota(jnp.int32, (SUBLANES, d), 0)
    state = state_ref[...]
    for gidx in range(rows // SUBLANES):
        r0 = gidx * SUBLANES
        av = a_ref[r0:r0 + SUBLANES, :]
        bv = b_ref[r0:r0 + SUBLANES, :]
        for s in (1, 2, 4):
            keep = sub >= s
            bv = jnp.where(keep, av * pltpu.roll(bv, s, axis=0) + bv, bv)
            av = jnp.where(keep, av * pltpu.roll(av, s, axis=0), av)
        hv = av * state + bv
        b_ref[r0:r0 + SUBLANES, :] = hv
        state = jnp.broadcast_to(hv[SUBLANES - 1:SUBLANES, :], (SUBLANES, d))
        if gidx % LRU_SCAN_GROUPS_PER_PIECE == LRU_SCAN_GROUPS_PER_PIECE - 1:
            yield
    state_ref[...] = state
    y_ref[...] = (b_ref[...] * gate).astype(y_ref.dtype)


def _mlp_stage(y, h, wo_ref, g_ref, wup_ref, wdn_ref, gf_ref, o_ref, hn_ref, final_norm, ff_chunk):
    d_ff = wup_ref.shape[1]
    h1 = h + _dot(y, wo_ref[...])
    hn_ref[...] = _rms_norm(h1, g_ref[...]).astype(_BF16)
    o_ref[...] = h1
    yield
    for c in range(0, d_ff, ff_chunk):
        up = jnp.maximum(_dot(hn_ref[...], wup_ref[:, c:c + ff_chunk].astype(_BF16)), 0.0)
        up = (up * up).astype(_BF16)
        yield
        o_ref[...] += _dot(up, wdn_ref[c:c + ff_chunk, :].astype(_BF16))
        yield
    if final_norm:
        o_ref[...] = _rms_norm(o_ref[...], gf_ref[...])


def _interleave(*stages):
    live = list(stages)
    while live:
        for stage in list(live):
            if next(stage, StopIteration) is StopIteration:
                live.remove(stage)


def _mlp_kernel(y_ref, h_ref, wo_ref, g_ref, wup_ref, wdn_ref, gf_ref, o_ref, hn_ref, *, final_norm):
    _interleave(_mlp_stage(y_ref[...], h_ref[...], wo_ref, g_ref, wup_ref, wdn_ref, gf_ref, o_ref, hn_ref,
                           final_norm, MLP_FF_CHUNK))


def _proj_mlp(y, h, w_o, g, w_up, w_down, g_final, final_norm):
    m, d = h.shape
    d_ff = w_up.shape[2]
    first_layer = lambda shape: pl.BlockSpec((None,) + shape, lambda i: (0, 0, 0), pipeline_mode=pl.Buffered(1))
    row_spec = pl.BlockSpec((MLP_ROWS, d), lambda i: (i, 0))
    return pl.pallas_call(
        functools.partial(_mlp_kernel, final_norm=final_norm),
        grid=(m // MLP_ROWS,),
        in_specs=[row_spec, row_spec, _resident((d, d)), _resident((1, d)), first_layer((d, d_ff)),
                  first_layer((d_ff, d)), _resident((1, d))],
        out_specs=row_spec,
        out_shape=jax.ShapeDtypeStruct((m, d), _F32),
        scratch_shapes=[pltpu.VMEM((MLP_ROWS, d), _BF16)],
        compiler_params=pltpu.CompilerParams(dimension_semantics=("parallel",), vmem_limit_bytes=VMEM_LIMIT),
        name="proj_mlp",
    )(y, h, w_o, g, w_up, w_down, g_final)


def _lru_layer_kernel(hcur_ref, hprev_ref, g_ref, win_ref, cw_ref, cb_ref, wg_ref, brg_ref, big_ref, lam_ref,
                      wo_ref, gm_ref, wup_ref, wdn_ref, gf_ref, out_ref,
                      rec_ref, a_ref, b_ref, state_ref, y_ref, hn_ref, obuf_ref, *, steps_per_batch, n_blocks):
    s = pl.program_id(0)
    rows = hcur_ref.shape[0]
    cur, prev = s % 2, 1 - s % 2

    @pl.when(s == 0)
    def _():
        y_ref[...] = jnp.zeros_like(y_ref)
        obuf_ref[...] = jnp.zeros_like(obuf_ref)

    @pl.when(s % steps_per_batch == 0)
    def _():
        rec_ref[0:SUBLANES, :] = jnp.zeros((SUBLANES, rec_ref.shape[1]), _F32)
        state_ref[...] = jnp.zeros_like(state_ref)

    @pl.when(s <= n_blocks)
    def _():
        y_prev = y_ref[...]
        _interleave(
            _lru_stage(hcur_ref, g_ref, win_ref, cw_ref, cb_ref, wg_ref, brg_ref, big_ref, lam_ref, y_ref,
                       rec_ref, a_ref, b_ref, state_ref),
            _mlp_stage(y_prev, hprev_ref[...], wo_ref, gm_ref, wup_ref, wdn_ref, gf_ref,
                       obuf_ref.at[cur], hn_ref, True, LRU_LAYER_FF_CHUNK))

    out_ref[0:rows - N_META, :] = obuf_ref[prev, N_META:rows, :]
    out_ref[rows - N_META:rows, :] = obuf_ref[cur, 0:N_META, :]


def _lru_layer(h, batch, seq, g, w_in, conv_w, conv_b, w_gates, b_rg, b_ig, lam, w_out, g_mlp, w_up, w_down,
               g_final):
    m, d = h.shape
    d_ff = w_up.shape[1]
    n_blocks = m // TIME_BLOCK
    steps_per_batch = n_blocks // batch
    cur_spec = pl.BlockSpec((TIME_BLOCK, d), lambda s: (jnp.minimum(s, n_blocks - 1), 0))
    prev_spec = pl.BlockSpec((TIME_BLOCK, d), lambda s: (jnp.clip(s - 1, 0, n_blocks - 1), 0))

    def out_index(s):
        block = jnp.maximum(s - 2, 0)
        return block // steps_per_batch, block % steps_per_batch, 0

    return pl.pallas_call(
        functools.partial(_lru_layer_kernel, steps_per_batch=steps_per_batch, n_blocks=n_blocks),
        grid=(n_blocks + 2,),
        in_specs=[cur_spec, prev_spec, _resident((1, d)), _resident((d, 2 * d)), _resident((CONV_WIDTH, d)),
                  _resident((1, d)), _resident((LRU_BLOCKS, LRU_BLOCK_DIM, 2 * LRU_BLOCK_DIM)),
                  _resident((1, d)), _resident((1, d)), _resident((1, d)),
                  _resident((d, d)), _resident((1, d)), _resident((d, d_ff)), _resident((d_ff, d)),
                  _resident((1, d))],
        out_specs=pl.BlockSpec((None, TIME_BLOCK, d), out_index),
        out_shape=jax.ShapeDtypeStruct((batch, seq, d), _F32),
        scratch_shapes=[pltpu.VMEM((SUBLANES + TIME_BLOCK, d), _F32), pltpu.VMEM((TIME_BLOCK, d), _F32),
                        pltpu.VMEM((TIME_BLOCK, d), _F32), pltpu.VMEM((SUBLANES, d), _F32),
                        pltpu.VMEM((TIME_BLOCK, d), _BF16), pltpu.VMEM((TIME_BLOCK, d), _BF16),
                        pltpu.VMEM((2, TIME_BLOCK, d), _F32)],
        compiler_params=pltpu.CompilerParams(dimension_semantics=("arbitrary",), vmem_limit_bytes=VMEM_LIMIT),
        name="lru_layer",
    )(h, h, g, w_in, conv_w, conv_b, w_gates, b_rg, b_ig, lam, w_out, g_mlp, w_up, w_down, g_final)


def kernel(x, meta_tokens, norm_mix, norm_mlp, sb_w_qkv, sb_w_o, lru_w_in, lru_conv_w, lru_conv_b, lru_w_rg,
           lru_b_rg, lru_w_ig, lru_b_ig, lru_lambda, lru_w_out, mlp_w_up, mlp_w_down, norm_final):
    b, seq, d = x.shape
    assert d == HEADS * HEAD_DIM == LRU_BLOCKS * LRU_BLOCK_DIM
    t_len = N_META + seq
    tp = -(-t_len // TIME_BLOCK) * TIME_BLOCK
    assert (b * tp) % MLP_ROWS == 0 and tp % ATTN_BLOCK == 0

    row = lambda v: v.reshape(1, d)
    q, kt, v, h = _qkv(x, meta_tokens.astype(x.dtype), tp, row(norm_mix[0]), sb_w_qkv[0])
    o = _sb_attention(q, kt, v)
    h = _proj_mlp(o.reshape(b * tp, d), h.reshape(b * tp, d), sb_w_o[0].astype(_BF16), row(norm_mlp[0]),
                  mlp_w_up, mlp_w_down, row(norm_final), False)

    w_gates = jnp.concatenate([lru_w_rg[0], lru_w_ig[0]], axis=-1).astype(_BF16)
    return _lru_layer(h, b, seq, row(norm_mix[1]), lru_w_in[0].astype(_BF16), lru_conv_w[0],
                      row(lru_conv_b[0]), w_gates, row(lru_b_rg[0]), row(lru_b_ig[0]), row(lru_lambda[0]),
                      lru_w_out[0].astype(_BF16), row(norm_mlp[1]), mlp_w_up[1].astype(_BF16),
                      mlp_w_down[1].astype(_BF16), row(norm_final))
```
